```python
import math
import jax, jax.numpy as jnp
from jax import lax
import numpy as np

D_MODEL = 1024
BATCH = 2
SEQ = 16384
DEPTH = 2

MIX_WIDTH = D_MODEL
GM_GROUPS = 4
GM_CH = MIX_WIDTH // 2 // GM_GROUPS
GM_CHUNK = 128
GM_WIDTH = GM_GROUPS * GM_CH
ATT_HEADS = 8
ATT_HD = MIX_WIDTH // 2 // ATT_HEADS
ATT_WIDTH = ATT_HEADS * ATT_HD
DILATED_PATTERNS = ((128, 1), (512, 4), (2048, 16))
ATT_BLOCK = 128
DIL_PAD = ATT_BLOCK * 16
IN_EVEN = 2 * GM_WIDTH + 3 * ATT_WIDTH + MIX_WIDTH
RW_HEAD = 64
RW_HEADS = D_MODEL // RW_HEAD
RW_DECAY_LORA = 64
RW_AAA_LORA = 64
RMS_EPS = 1e-6
LNX_EPS = 64e-5
N_EVEN = (DEPTH + 1) // 2
N_ODD = DEPTH // 2

kernel_name = "hybrid_gmlp_dilated_alibi_rwkv7"


def rmsnorm(x, g):
    xf = x.astype(jnp.float32)
    y = xf * lax.rsqrt(jnp.mean(xf * xf, axis=-1, keepdims=True) + RMS_EPS)
    return (y * g.astype(jnp.float32)).astype(x.dtype)


def alibi_slopes(n_heads):
    s = 2.0 ** (-8.0 * np.arange(1, n_heads + 1) / n_heads)
    return jnp.asarray(s, dtype=jnp.float32)


def dilated_window_attention(q, k, v, window, dilation):
    B, S, H, hd = q.shape
    L = S // dilation
    nblk = L // ATT_BLOCK
    reach = window // dilation

    def blocks(t):
        return t.reshape(B, nblk, ATT_BLOCK, dilation, H, hd)

    def with_prev(t):
        prev = jnp.concatenate([jnp.zeros_like(t[:, :1]), t[:, :-1]], axis=1)
        return jnp.concatenate([prev, t], axis=2)

    qb = blocks(q)
    kc = with_prev(blocks(k))
    vc = with_prev(blocks(v))
    s = jnp.einsum('bnqrhc,bnkrhc->bnrhqk', qb, kc).astype(jnp.float32) / math.sqrt(hd)

    qi = jnp.arange(ATT_BLOCK)[:, None]
    kloc = jnp.arange(2 * ATT_BLOCK)[None, :] - ATT_BLOCK
    j = qi - kloc
    blk = jnp.arange(nblk)[:, None, None]
    valid = (j >= 0) & (j <= reach) & (blk * ATT_BLOCK + kloc[None] >= 0)
    bias = -alibi_slopes(H)[:, None, None] * (j * dilation).astype(jnp.float32)
    s = jnp.where(valid[None, :, None, None], s + bias[None, None, None], -jnp.inf)
    m = jnp.max(s, axis=-1, keepdims=True)
    p = jnp.exp(s - m)
    l = jnp.sum(p, axis=-1)
    o = jnp.einsum('bnrhqk,bnkrhc->bnqrhc', p, vc.astype(jnp.float32))
    l_t = l.transpose(0, 1, 4, 2, 3)
    o = (o / l_t[..., None]).reshape(B, S, H, hd)
    m_t = m[..., 0].transpose(0, 1, 4, 2, 3).reshape(B, S, H)
    return o, m_t, l_t.reshape(B, S, H)


def mixed_dilated_attention(q, k, v):
    B, S, H, hd = q.shape
    s_pad = -(-S // DIL_PAD) * DIL_PAD
    pad = ((0, 0), (0, s_pad - S), (0, 0), (0, 0))
    qp, kp, vp = jnp.pad(q, pad), jnp.pad(k, pad), jnp.pad(v, pad)
    outs = [dilated_window_attention(qp, kp, vp, w, d) for (w, d) in DILATED_PATTERNS]
    m_all = jnp.stack([o[1] for o in outs], axis=0)
    l_all = jnp.stack([o[2] for o in outs], axis=0)
    o_all = jnp.stack([o[0] for o in outs], axis=0)
    wts = l_all * jnp.exp(m_all - jnp.max(m_all, axis=0, keepdims=True))
    out = jnp.sum(wts[..., None] * o_all, axis=0) / jnp.sum(wts, axis=0)[..., None]
    return out[:, :S].astype(q.dtype)


def even_mixer(h, w_in, gm_norm, gm_ws, gm_b, w_out):
    B, S, _ = h.shape
    proj = h @ w_in
    splits = list(np.cumsum([GM_WIDTH, GM_WIDTH, ATT_WIDTH, ATT_WIDTH, ATT_WIDTH]))
    u, va, q, k, vb, z = jnp.split(proj, splits, axis=-1)
    u = jax.nn.gelu(u)
    va = rmsnorm(jax.nn.gelu(va).reshape(B, S, GM_GROUPS, GM_CH), gm_norm)
    vch = va.reshape(B, S // GM_CHUNK, GM_CHUNK, GM_GROUPS, GM_CH)
    ws = gm_ws * jnp.tril(jnp.ones((GM_CHUNK, GM_CHUNK), dtype=gm_ws.dtype))[None]
    spatial = jnp.einsum('gts,bnsgc->bntgc', ws, vch) + gm_b.T[:, :, None]
    a_out = u * spatial.reshape(B, S, GM_WIDTH)
    heads = lambda t: t.reshape(B, S, ATT_HEADS, ATT_HD)
    b_out = mixed_dilated_attention(heads(q), heads(k), heads(vb)).reshape(B, S, ATT_WIDTH)
    y = jnp.concatenate([a_out, b_out], axis=-1) * jax.nn.silu(z)
    return y @ w_out


def rwkv7_scan(r, w, k, v, kk, a):
    _, B, H, N = r.shape

    def step(state, inp):
        r_t, w_t, k_t, v_t, kk_t, a_t = inp
        sa = jnp.einsum('bhvk,bhk->bhv', state, -kk_t)
        state = (state * w_t[:, :, None, :]
                 + sa[..., None] * (kk_t * a_t)[:, :, None, :]
                 + v_t[..., None] * k_t[:, :, None, :])
        y_t = jnp.einsum('bhvk,bhk->bhv', state, r_t)
        return state, y_t

    state0 = jnp.zeros((B, H, N, N), dtype=jnp.float32)
    _, ys = lax.scan(step, state0, (r, w, k, v, kk, a))
    return ys


def odd_mixer(h, mu, w_r, w_k, w_v, w_g, w0, w1, w2, a0, a1, a2, k_k, k_a, r_k, lnx_w, lnx_b, w_o):
    B, S, D = h.shape
    f32 = jnp.float32
    xx = jnp.concatenate([jnp.zeros_like(h[:, :1]), h[:, :-1]], axis=1) - h
    xr, xw, xk, xv, xa, xg = [h + xx * mu[i] for i in range(6)]
    r = xr @ w_r
    k = xk @ w_k
    v = xv @ w_v
    g = xg @ w_g
    logw = -jax.nn.softplus(-(w0 + jnp.tanh(xw @ w1) @ w2).astype(f32)) - 0.5
    decay = jnp.exp(-jnp.exp(logw))
    a = jax.nn.sigmoid((a0 + (xa @ a1) @ a2).astype(f32))
    hs = lambda t: t.astype(f32).reshape(B, S, RW_HEADS, RW_HEAD)
    kk = hs(k * k_k)
    kk = kk * lax.rsqrt(jnp.maximum(jnp.sum(kk * kk, axis=-1, keepdims=True), 1e-24))
    k = k.astype(f32) * (1.0 + (a - 1.0) * k_a.astype(f32))
    rh, kh, vh, wh, ah = hs(r), hs(k), hs(v), hs(decay), hs(a)
    tm = lambda t: t.transpose(1, 0, 2, 3)
    ys = rwkv7_scan(tm(rh), tm(wh), tm(kh), tm(vh), tm(kk), tm(ah)).transpose(1, 0, 2, 3)
    mean = jnp.mean(ys, axis=-1, keepdims=True)
    var = jnp.mean((ys - mean) ** 2, axis=-1, keepdims=True)
    y = ((ys - mean) * lax.rsqrt(var + LNX_EPS)).reshape(B, S, D) * lnx_w.astype(f32) + lnx_b.astype(f32)
    bonus = jnp.sum(rh * kh * r_k.astype(f32), axis=-1, keepdims=True) * vh
    y = (y + bonus.reshape(B, S, D)).astype(h.dtype) * jax.nn.silu(g)
    return y @ w_o


def setup_inputs(seed: int = 0) -> dict:
    key = jax.random.key(seed)
    ks = iter(jax.random.split(key, 40))
    nrm = lambda shape, scale: scale * jax.random.normal(next(ks), shape, dtype=jnp.float32)
    D, Ne, No, H, N = D_MODEL, N_EVEN, N_ODD, RW_HEADS, RW_HEAD
    return {
        "x": nrm((BATCH, SEQ, D), 1.0),
        "ln_even": 1.0 + nrm((Ne, D), 0.02),
        "w_in_even": nrm((Ne, D, IN_EVEN), D ** -0.5),
        "gm_norm": 1.0 + nrm((Ne, GM_GROUPS, GM_CH), 0.02),
        "gm_ws": nrm((Ne, GM_GROUPS, GM_CHUNK, GM_CHUNK), 0.5 * GM_CHUNK ** -0.5),
        "gm_b": 1.0 + nrm((Ne, GM_GROUPS, GM_CHUNK), 0.1),
        "w_out_even": nrm((Ne, MIX_WIDTH, D), 0.5 * MIX_WIDTH ** -0.5),
        "ln_odd": 1.0 + nrm((No, D), 0.02),
        "rw_mu": jax.random.uniform(next(ks), (No, 6, D), dtype=jnp.float32),
        "rw_wr": nrm((No, D, D), D ** -0.5),
        "rw_wk": nrm((No, D, D), D ** -0.5),
        "rw_wv": nrm((No, D, D), D ** -0.5),
        "rw_wg": nrm((No, D, D), D ** -0.5),
        "rw_w0": jax.random.uniform(next(ks), (No, D), dtype=jnp.float32, minval=-5.0, maxval=1.0),
        "rw_w1": nrm((No, D, RW_DECAY_LORA), D ** -0.5),
        "rw_w2": nrm((No, RW_DECAY_LORA, D), 0.1 * RW_DECAY_LORA ** -0.5),
        "rw_a0": nrm((No, D), 0.1),
        "rw_a1": nrm((No, D, RW_AAA_LORA), D ** -0.5),
        "rw_a2": nrm((No, RW_AAA_LORA, D), 0.1 * RW_AAA_LORA ** -0.5),
        "rw_kk": 0.85 + nrm((No, D), 0.02),
        "rw_ka": 1.0 + nrm((No, D), 0.02),
        "rw_rk": nrm((No, H, N), 0.1),
        "rw_lnw": 1.0 + nrm((No, D), 0.02),
        "rw_lnb": nrm((No, D), 0.01),
        "rw_wo": nrm((No, D, D), 0.5 * D ** -0.5),
        "final_norm": 1.0 + nrm((D,), 0.02),
    }


def reference(x, ln_even, w_in_even, gm_norm, gm_ws, gm_b, w_out_even,
              ln_odd, rw_mu, rw_wr, rw_wk, rw_wv, rw_wg, rw_w0, rw_w1, rw_w2,
              rw_a0, rw_a1, rw_a2, rw_kk, rw_ka, rw_rk, rw_lnw, rw_lnb, rw_wo,
              final_norm):
    for layer in range(DEPTH):
        i = layer // 2
        if layer % 2 == 0:
            h = rmsnorm(x, ln_even[i])
            x = x + even_mixer(h, w_in_even[i], gm_norm[i], gm_ws[i], gm_b[i], w_out_even[i])
        else:
            h = rmsnorm(x, ln_odd[i])
            x = x + odd_mixer(h, rw_mu[i], rw_wr[i], rw_wk[i], rw_wv[i], rw_wg[i],
                              rw_w0[i], rw_w1[i], rw_w2[i], rw_a0[i], rw_a1[i], rw_a2[i],
                              rw_kk[i], rw_ka[i], rw_rk[i], rw_lnw[i], rw_lnb[i], rw_wo[i])
    return rmsnorm(x, final_norm)
```

```python
import functools
import math

import numpy as np
import jax
import jax.numpy as jnp
from jax import lax
from jax.experimental import pallas as pl
from jax.experimental.pallas import tpu as pltpu

F32 = jnp.float32
BF16 = jnp.bfloat16

D_MODEL = 1024
GM_GROUPS = 4
GM_CH = 128
GM_CHUNK = 128
GM_WIDTH = GM_GROUPS * GM_CH
ATT_HEADS = 8
ATT_HD = 64
ATT_WIDTH = ATT_HEADS * ATT_HD
ATT_BLOCK = 128
DILATIONS = (1, 4, 16)
ATT_TILE = ATT_BLOCK * 16
ATT_SLABS = ATT_WIDTH // 128
IN_EVEN = 2 * GM_WIDTH + 3 * ATT_WIDTH + D_MODEL
RW_HEAD = 64
RW_HEADS = D_MODEL // RW_HEAD
RW_LORA = 64
RMS_EPS = 1e-6
LNX_EPS = 64e-5

LANES = 128
TOKEN_TILE = 256
SCAN_CHUNK = 64
SCAN_VHI = RW_HEAD // 4
VMEM_LIMIT = 56 * 1024 * 1024


def _cparams(sem):
    return pltpu.CompilerParams(dimension_semantics=sem, vmem_limit_bytes=VMEM_LIMIT)


def _gelu_tanh(x):
    return 0.5 * x * (1.0 + jnp.tanh(0.7978845608028654 * (x + 0.044715 * (x * x * x))))


def _silu(x):
    return x * jax.nn.sigmoid(x)


def _rms(x, g):
    return x * lax.rsqrt(jnp.mean(x * x, axis=-1, keepdims=True) + RMS_EPS) * g


def _even_in_kernel(x_ref, ln_ref, win_ref, gmn_ref, ws_ref, gmb_ref,
                    ya_ref, zb_ref, q_ref, k_ref, v_ref):
    tm = x_ref.shape[1]
    h = _rms(x_ref[0], ln_ref[...])
    proj = jnp.dot(h.astype(BF16), win_ref[...], preferred_element_type=F32)
    row = lax.broadcasted_iota(jnp.int32, (GM_CHUNK, GM_CHUNK), 0)
    col = lax.broadcasted_iota(jnp.int32, (GM_CHUNK, GM_CHUNK), 1)
    causal = row >= col
    z0 = 2 * GM_WIDTH + 3 * ATT_WIDTH
    sz = _silu(proj[:, z0:z0 + D_MODEL])
    for g in range(GM_GROUPS):
        u = _gelu_tanh(proj[:, g * GM_CH:(g + 1) * GM_CH])
        vg = _gelu_tanh(proj[:, GM_WIDTH + g * GM_CH:GM_WIDTH + (g + 1) * GM_CH])
        vn = vg * lax.rsqrt(jnp.mean(vg * vg, axis=-1, keepdims=True) + RMS_EPS) * gmn_ref[g:g + 1, :]
        wsg = jnp.where(causal, ws_ref[g], 0.0).astype(BF16)
        for c in range(tm // GM_CHUNK):
            rows = slice(c * GM_CHUNK, (c + 1) * GM_CHUNK)
            sp = jnp.dot(wsg, vn[rows].astype(BF16), preferred_element_type=F32) + gmb_ref[g]
            a = u[rows] * sp
            ya_ref[0, rows, g * GM_CH:(g + 1) * GM_CH] = (a * sz[rows, g * GM_CH:(g + 1) * GM_CH]).astype(BF16)
    q0 = 2 * GM_WIDTH
    for j in range(ATT_SLABS):
        lanes = slice(j * LANES, (j + 1) * LANES)
        zb_ref[0, j] = sz[:, GM_WIDTH + j * LANES:GM_WIDTH + (j + 1) * LANES]
        q_ref[0, j] = proj[:, q0:q0 + ATT_WIDTH][:, lanes] * (1.0 / math.sqrt(ATT_HD))
        k_ref[0, j] = proj[:, q0 + ATT_WIDTH:q0 + 2 * ATT_WIDTH][:, lanes]
        v_ref[0, j] = proj[:, q0 + 2 * ATT_WIDTH:q0 + 3 * ATT_WIDTH][:, lanes]


def _even_in(x, ln, w_in, gm_norm, gm_ws, gm_b):
    bsz, seq, _ = x.shape
    tm = TOKEN_TILE
    slab = jax.ShapeDtypeStruct((bsz, ATT_SLABS, seq, LANES), F32)
    slab_spec = pl.BlockSpec((1, ATT_SLABS, tm, LANES), lambda b, i: (b, 0, i, 0))
    const = lambda *shape: pl.BlockSpec(shape, lambda b, i: (0,) * len(shape))
    gmb = jnp.broadcast_to(gm_b[:, :, None], (GM_GROUPS, GM_CHUNK, GM_CH))
    return pl.pallas_call(
        _even_in_kernel,
        grid=(bsz, seq // tm),
        in_specs=[
            pl.BlockSpec((1, tm, D_MODEL), lambda b, i: (b, i, 0)),
            const(1, D_MODEL),
            const(D_MODEL, IN_EVEN),
            const(GM_GROUPS, GM_CH),
            const(GM_GROUPS, GM_CHUNK, GM_CHUNK),
            const(GM_GROUPS, GM_CHUNK, GM_CH),
        ],
        out_specs=[
            pl.BlockSpec((1, tm, GM_WIDTH), lambda b, i: (b, i, 0)),
            slab_spec, slab_spec, slab_spec, slab_spec,
        ],
        out_shape=[jax.ShapeDtypeStruct((bsz, seq, GM_WIDTH), BF16), slab, slab, slab, slab],
        compiler_params=_cparams(("parallel", "parallel")),
        name="even_in",
    )(x, ln.reshape(1, D_MODEL), w_in.astype(BF16), gm_norm, gm_ws, gmb)


def _attn_table():
    qi = np.arange(ATT_BLOCK)[:, None]
    kloc = np.arange(2 * ATT_BLOCK)[None, :] - ATT_BLOCK
    j = qi - kloc
    valid = (j >= 0) & (j <= ATT_BLOCK)
    slopes = 2.0 ** (-8.0 * np.arange(1, ATT_HEADS + 1) / ATT_HEADS)
    tabs = []
    for d in DILATIONS:
        bias = -slopes[:, None, None].astype(np.float32) * (j * d).astype(np.float32)[None]
        tabs.append(np.where(valid[None], bias, -np.inf))
    return jnp.asarray(np.stack(tabs), dtype=F32)


def _attn_kernel(q_ref, kp_ref, kc_ref, vp_ref, vc_ref, tab_ref, o_ref, kbuf, vbuf, acc, mm, ll):
    tile = pl.program_id(2)
    kbuf[0:ATT_TILE, :] = kp_ref[0, 0]
    kbuf[ATT_TILE:2 * ATT_TILE, :] = kc_ref[0, 0]
    vbuf[0:ATT_TILE, :] = vp_ref[0, 0]
    vbuf[ATT_TILE:2 * ATT_TILE, :] = vc_ref[0, 0]
    head0_q = lax.broadcasted_iota(jnp.int32, (ATT_BLOCK, LANES), 1) < ATT_HD
    head0_kv = lax.broadcasted_iota(jnp.int32, (2 * ATT_BLOCK, LANES), 1) < ATT_HD
    prev_cols = lax.broadcasted_iota(jnp.int32, (ATT_BLOCK, 2 * ATT_BLOCK), 1) < ATT_BLOCK

    for p, d in enumerate(DILATIONS):
        shift = int(math.log2(d))

        def block(i, carry, p=p, d=d, shift=shift):
            nl = lax.shift_right_logical(i, shift)
            r = jnp.bitwise_and(i, d - 1)
            qs = nl * (ATT_BLOCK * d) + r
            ks = qs + (ATT_TILE - ATT_BLOCK * d)
            if d == 1:
                qsl, ksl = pl.ds(qs, ATT_BLOCK), pl.ds(ks, 2 * ATT_BLOCK)
            else:
                qsl, ksl = pl.ds(qs, ATT_BLOCK, stride=d), pl.ds(ks, 2 * ATT_BLOCK, stride=d)
            qf = q_ref[0, 0, qsl, :]
            kf = kbuf[ksl, :].astype(BF16)
            vf = vbuf[ksl, :]
            pen = jnp.where(jnp.logical_and(tile == 0, nl == 0), -jnp.inf, 0.0).astype(F32)
            first_mask = jnp.where(prev_cols, pen, 0.0)
            o_tot = jnp.zeros((ATT_BLOCK, LANES), F32)
            m_b = jnp.zeros((ATT_BLOCK, LANES), F32)
            l_b = jnp.zeros((ATT_BLOCK, LANES), F32)
            for hh in range(2):
                hq = head0_q if hh == 0 else jnp.logical_not(head0_q)
                hkv = head0_kv if hh == 0 else jnp.logical_not(head0_kv)
                qm = jnp.where(hq, qf, 0.0).astype(BF16)
                s = lax.dot_general(qm, kf, (((1,), (1,)), ((), ())), preferred_element_type=F32)
                s = s + tab_ref[p, hh] + first_mask
                m = jnp.max(s, axis=-1, keepdims=True)
                e = jnp.exp(s - m)
                l = jnp.sum(e, axis=-1, keepdims=True)
                vm = jnp.where(hkv, vf, 0.0).astype(BF16)
                o_tot = o_tot + jnp.dot(e.astype(BF16), vm, preferred_element_type=F32)
                m_b = jnp.where(hq, m, m_b)
                l_b = jnp.where(hq, l, l_b)
            acc[p, qsl, :] = o_tot
            mm[p, qsl, :] = m_b
            ll[p, qsl, :] = l_b
            return carry

        lax.fori_loop(0, ATT_TILE // ATT_BLOCK, block, 0)

    m_all = jnp.maximum(jnp.maximum(mm[0], mm[1]), mm[2])
    num = jnp.zeros((ATT_TILE, LANES), F32)
    den = jnp.zeros((ATT_TILE, LANES), F32)
    for p in range(len(DILATIONS)):
        e = jnp.exp(mm[p] - m_all)
        num = num + e * acc[p]
        den = den + e * ll[p]
    o_ref[0, 0] = num / den


def _attention(q, k, v):
    bsz, _, seq, _ = q.shape
    blk = (1, 1, ATT_TILE, LANES)
    cur = pl.BlockSpec(blk, lambda b, j, t: (b, j, t, 0))
    prev = pl.BlockSpec(blk, lambda b, j, t: (b, j, jnp.maximum(t - 1, 0), 0))
    n_pat = len(DILATIONS)
    return pl.pallas_call(
        _attn_kernel,
        grid=(bsz, ATT_SLABS, seq // ATT_TILE),
        in_specs=[cur, prev, cur, prev, cur,
                  pl.BlockSpec((n_pat, 2, ATT_BLOCK, 2 * ATT_BLOCK), lambda b, j, t: (0, j, 0, 0))],
        out_specs=cur,
        out_shape=jax.ShapeDtypeStruct(q.shape, F32),
        scratch_shapes=[
            pltpu.VMEM((2 * ATT_TILE, LANES), F32),
            pltpu.VMEM((2 * ATT_TILE, LANES), F32),
            pltpu.VMEM((n_pat, ATT_TILE, LANES), F32),
            pltpu.VMEM((n_pat, ATT_TILE, LANES), F32),
            pltpu.VMEM((n_pat, ATT_TILE, LANES), F32),
        ],
        compiler_params=_cparams(("parallel", "parallel", "parallel")),
        name="attention",
    )(q, k, k, v, v, _attn_table())


def _even_out_kernel(x_ref, ya_ref, bo_ref, zb_ref, wout_ref, o_ref):
    parts = [ya_ref[0]] + [(bo_ref[0, j] * zb_ref[0, j]).astype(BF16) for j in range(ATT_SLABS)]
    y = jnp.concatenate(parts, axis=1)
    o_ref[0] = x_ref[0] + jnp.dot(y, wout_ref[...], preferred_element_type=F32)


def _even_out(x, ya, b_out, zb, w_out):
    bsz, seq, _ = x.shape
    tm = TOKEN_TILE
    row = pl.BlockSpec((1, tm, D_MODEL), lambda b, i: (b, i, 0))
    slab_spec = pl.BlockSpec((1, ATT_SLABS, tm, LANES), lambda b, i: (b, 0, i, 0))
    return pl.pallas_call(
        _even_out_kernel,
        grid=(bsz, seq // tm),
        in_specs=[row, pl.BlockSpec((1, tm, GM_WIDTH), lambda b, i: (b, i, 0)), slab_spec, slab_spec,
                  pl.BlockSpec((D_MODEL, D_MODEL), lambda b, i: (0, 0))],
        out_specs=row,
        out_shape=jax.ShapeDtypeStruct(x.shape, F32),
        compiler_params=_cparams(("parallel", "parallel")),
        name="even_out",
    )(x, ya, b_out, zb, w_out.astype(BF16))


def _index_major(n=RW_HEAD, heads=RW_HEADS):
    new = np.arange(n * heads)
    return (new % heads) * n + new // heads


def _head_sum(t):
    s = t[:, 0:LANES]
    for c in range(1, D_MODEL // LANES):
        s = s + t[:, c * LANES:(c + 1) * LANES]
    for sh in (RW_HEADS, 2 * RW_HEADS, 4 * RW_HEADS):
        s = s + pltpu.roll(s, sh, 1)
    return s


def _tile8(s):
    return jnp.concatenate([s] * (D_MODEL // LANES), axis=1)


def _odd_in_kernel(x_ref, halo_ref, ln_ref, mu_ref, wr_ref, wk_ref, wv_ref, wg_ref,
                   w1_ref, w2_ref, a1_ref, a2_ref, w0_ref, a0_ref, kk_ref, ka_ref, rk_ref,
                   r_out, w_out, k_out, v_out, a_out, b_out, g_out, bonus_out):
    tm = x_ref.shape[1]
    ln = ln_ref[...]
    h = _rms(x_ref[0], ln)
    h_halo = _rms(halo_ref[0], ln)
    last = jnp.where(pl.program_id(1) == 0, 0.0, h_halo[7:8, :])
    first_row = lax.broadcasted_iota(jnp.int32, (tm, D_MODEL), 0) == 0
    hprev = jnp.where(first_row, last, pltpu.roll(h, 1, 0))
    xx = hprev - h
    mix = lambda i: (h + xx * mu_ref[i:i + 1, :]).astype(BF16)
    dot = lambda a, b: jnp.dot(a, b, preferred_element_type=F32)
    r = dot(mix(0), wr_ref[...])
    lw = dot(jnp.tanh(dot(mix(1), w1_ref[...])).astype(BF16), w2_ref[...])
    k = dot(mix(2), wk_ref[...])
    v = dot(mix(3), wv_ref[...])
    la = dot(dot(mix(4), a1_ref[...]).astype(BF16), a2_ref[...])
    g = dot(mix(5), wg_ref[...])
    z = -(w0_ref[...] + lw)
    softplus = jnp.maximum(z, 0.0) + jnp.log1p(jnp.exp(-jnp.abs(z)))
    decay = jnp.exp(-jnp.exp(-softplus - 0.5))
    a = jax.nn.sigmoid(a0_ref[...] + la)
    kk = k * kk_ref[...]
    kk = kk * _tile8(lax.rsqrt(jnp.maximum(_head_sum(kk * kk), 1e-24)))
    kmod = k * (1.0 + (a - 1.0) * ka_ref[...])
    bonus = _tile8(_head_sum(r * kmod * rk_ref[...])) * v
    r_out[0] = r
    w_out[0] = decay
    k_out[0] = kmod
    v_out[0] = v
    a_out[0] = -kk
    b_out[0] = kk * a
    g_out[0] = _silu(g)
    bonus_out[0] = bonus


def _odd_in(x1, ln, mu, wr, wk, wv, wg, w0, w1, w2, a0, a1, a2, k_k, k_a, r_k):
    bsz, seq, _ = x1.shape
    tm = TOKEN_TILE
    perm = _index_major()
    pw = lambda w: w[:, perm].astype(BF16)
    pv = lambda p: p.reshape(-1)[perm].reshape(1, D_MODEL)
    row = pl.BlockSpec((1, tm, D_MODEL), lambda b, i: (b, i, 0))
    halo = pl.BlockSpec((1, 8, D_MODEL), lambda b, i: (b, jnp.maximum(i * (tm // 8) - 1, 0), 0))
    const = lambda *shape: pl.BlockSpec(shape, lambda b, i: (0,) * len(shape))
    mat, vec = const(D_MODEL, D_MODEL), const(1, D_MODEL)
    out = jax.ShapeDtypeStruct(x1.shape, F32)
    return pl.pallas_call(
        _odd_in_kernel,
        grid=(bsz, seq // tm),
        in_specs=[row, halo, vec, const(6, D_MODEL), mat, mat, mat, mat,
                  const(D_MODEL, RW_LORA), const(RW_LORA, D_MODEL),
                  const(D_MODEL, RW_LORA), const(RW_LORA, D_MODEL),
                  vec, vec, vec, vec, vec],
        out_specs=[row] * 8,
        out_shape=[out] * 8,
        compiler_params=_cparams(("parallel", "arbitrary")),
        name="odd_in",
    )(x1, x1, ln.reshape(1, D_MODEL), mu, pw(wr), pw(wk), pw(wv), pw(wg),
      w1.astype(BF16), pw(w2), a1.astype(BF16), pw(a2),
      pv(w0), pv(a0), pv(k_k), pv(k_a), pv(r_k))


def _scan_kernel(w_ref, b_ref, k_ref, r_ref, an_ref, v_ref, y_ref, st, sa_scr):
    tc = v_ref.shape[0]

    @pl.when(pl.program_id(0) == 0)
    def _():
        st[...] = jnp.zeros(st.shape, F32)
        sa_scr[...] = jnp.zeros(sa_scr.shape, F32)

    def step(t, sa):
        vt = v_ref[t]
        accy = [jnp.zeros((SCAN_VHI, LANES), F32) for _ in range(2)]
        accs = [jnp.zeros((SCAN_VHI, LANES), F32) for _ in range(2)]
        for k in range(RW_HEAD):
            row = lambda ref: ref[t, k:k + 1, :]
            s = st[k] * row(w_ref) + sa * row(b_ref) + vt * row(k_ref)
            st[k] = s
            accy[k % 2] = accy[k % 2] + s * row(r_ref)
            accs[k % 2] = accs[k % 2] + s * row(an_ref)
        y_ref[t] = accy[0] + accy[1]
        return accs[0] + accs[1]

    sa_scr[...] = lax.fori_loop(0, tc, step, sa_scr[...])


def _rep_k(x):
    bsz, seq, _ = x.shape
    t = x.reshape(bsz, seq, RW_HEAD, RW_HEADS).transpose(1, 2, 0, 3)
    t = jnp.broadcast_to(t[:, :, None], (seq, RW_HEAD, LANES // (bsz * RW_HEADS), bsz, RW_HEADS))
    return t.reshape(seq, RW_HEAD, LANES)


def _scan(r, w, k, v, a, b):
    bsz, seq, _ = r.shape
    tc = SCAN_CHUNK
    a_next = jnp.concatenate([a[:, 1:], jnp.zeros_like(a[:, :1])], axis=1)
    v3 = v.reshape(bsz, seq, SCAN_VHI, 4, RW_HEADS).transpose(1, 2, 3, 0, 4).reshape(seq, SCAN_VHI, LANES)
    rep = pl.BlockSpec((tc, RW_HEAD, LANES), lambda i: (i, 0, 0))
    tile3 = pl.BlockSpec((tc, SCAN_VHI, LANES), lambda i: (i, 0, 0))
    y3 = pl.pallas_call(
        _scan_kernel,
        grid=(seq // tc,),
        in_specs=[rep, rep, rep, rep, rep, tile3],
        out_specs=tile3,
        out_shape=jax.ShapeDtypeStruct((seq, SCAN_VHI, LANES), F32),
        scratch_shapes=[pltpu.VMEM((RW_HEAD, SCAN_VHI, LANES), F32), pltpu.VMEM((SCAN_VHI, LANES), F32)],
        compiler_params=_cparams(("arbitrary",)),
        name="scan",
    )(_rep_k(w), _rep_k(b), _rep_k(k), _rep_k(r), _rep_k(a_next), v3)
    y = y3.reshape(seq, SCAN_VHI, 4, bsz, RW_HEADS).transpose(3, 0, 1, 2, 4)
    return y.reshape(bsz, seq, D_MODEL)


def _odd_out_kernel(y_ref, bonus_ref, g_ref, x_ref, lnw_ref, lnb_ref, wo_ref, fn_ref, o_ref):
    y = y_ref[0]
    inv_n = 1.0 / RW_HEAD
    dlt = y - _tile8(_head_sum(y)) * inv_n
    var = _tile8(_head_sum(dlt * dlt)) * inv_n
    yn = dlt * lax.rsqrt(var + LNX_EPS) * lnw_ref[...] + lnb_ref[...]
    z = ((yn + bonus_ref[0]) * g_ref[0]).astype(BF16)
    x2 = x_ref[0] + jnp.dot(z, wo_ref[...], preferred_element_type=F32)
    o_ref[0] = _rms(x2, fn_ref[...])


def _odd_out(y, bonus, gs, x1, lnw, lnb, wo, fn):
    bsz, seq, _ = x1.shape
    tm = TOKEN_TILE
    perm = _index_major()
    pv = lambda p: p.reshape(-1)[perm].reshape(1, D_MODEL)
    row = pl.BlockSpec((1, tm, D_MODEL), lambda b, i: (b, i, 0))
    vec = pl.BlockSpec((1, D_MODEL), lambda b, i: (0, 0))
    return pl.pallas_call(
        _odd_out_kernel,
        grid=(bsz, seq // tm),
        in_specs=[row, row, row, row, vec, vec, pl.BlockSpec((D_MODEL, D_MODEL), lambda b, i: (0, 0)), vec],
        out_specs=row,
        out_shape=jax.ShapeDtypeStruct(x1.shape, F32),
        compiler_params=_cparams(("parallel", "parallel")),
        name="odd_out",
    )(y, bonus, gs, x1, pv(lnw), pv(lnb), wo[perm, :].astype(BF16), fn.reshape(1, D_MODEL))


def kernel(x, ln_even, w_in_even, gm_norm, gm_ws, gm_b, w_out_even, ln_odd, rw_mu, rw_wr, rw_wk,
           rw_wv, rw_wg, rw_w0, rw_w1, rw_w2, rw_a0, rw_a1, rw_a2, rw_kk, rw_ka, rw_rk, rw_lnw,
           rw_lnb, rw_wo, final_norm):
    assert x.shape[1] % ATT_TILE == 0 and x.shape[2] == D_MODEL
    assert ln_even.shape[0] == 1 and ln_odd.shape[0] == 1
    ya, zb, q, k, v = _even_in(x, ln_even[0], w_in_even[0], gm_norm[0], gm_ws[0], gm_b[0])
    b_out = _attention(q, k, v)
    x1 = _even_out(x, ya, b_out, zb, w_out_even[0])
    r, w, kmod, vv, a, b, gs, bonus = _odd_in(
        x1, ln_odd[0], rw_mu[0], rw_wr[0], rw_wk[0], rw_wv[0], rw_wg[0], rw_w0[0], rw_w1[0], rw_w2[0],
        rw_a0[0], rw_a1[0], rw_a2[0], rw_kk[0], rw_ka[0], rw_rk[0])
    y = _scan(r, w, kmod, vv, a, b)
    return _odd_out(y, bonus, gs, x1, rw_lnw[0], rw_lnb[0], rw_wo[0], final_norm)
```

```python
import functools
import math

import numpy as np
import jax
import jax.numpy as jnp
from jax import lax
from jax.experimental import pallas as pl
from jax.experimental.pallas import tpu as pltpu

F32 = jnp.float32
BF16 = jnp.bfloat16

D_MODEL = 1024
GM_GROUPS = 4
GM_CH = 128
GM_CHUNK = 128
GM_WIDTH = GM_GROUPS * GM_CH
ATT_HEADS = 8
ATT_HD = 64
ATT_WIDTH = ATT_HEADS * ATT_HD
ATT_BLOCK = 128
DILATIONS = (1, 4, 16)
ATT_TILE = ATT_BLOCK * 16
ATT_SLABS = ATT_WIDTH // 128
ATT_UNROLL = 4
IN_EVEN = 2 * GM_WIDTH + 3 * ATT_WIDTH + D_MODEL
RW_HEAD = 64
RW_HEADS = D_MODEL // RW_HEAD
RW_LORA = 64
RMS_EPS = 1e-6
LNX_EPS = 64e-5

LANES = 128
TOKEN_TILE = 256
SCAN_CHUNK = 128
SCAN_VHI = RW_HEAD // 4
SCAN_SLABS = D_MODEL // LANES
VMEM_LIMIT = 56 * 1024 * 1024


def _cparams(sem):
    return pltpu.CompilerParams(dimension_semantics=sem, vmem_limit_bytes=VMEM_LIMIT)


def _gelu_tanh(x):
    return 0.5 * x * (1.0 + jnp.tanh(0.7978845608028654 * (x + 0.044715 * (x * x * x))))


def _silu(x):
    return x * jax.nn.sigmoid(x)


def _rms(x, g):
    return x * lax.rsqrt(jnp.mean(x * x, axis=-1, keepdims=True) + RMS_EPS) * g


def _even_in_kernel(x_ref, ln_ref, win_ref, gmn_ref, ws_ref, gmb_ref,
                    ya_ref, zb_ref, q_ref, k_ref, v_ref):
    tm = x_ref.shape[1]
    h = _rms(x_ref[0], ln_ref[...])
    proj = jnp.dot(h.astype(BF16), win_ref[...], preferred_element_type=F32)
    row = lax.broadcasted_iota(jnp.int32, (GM_CHUNK, GM_CHUNK), 0)
    col = lax.broadcasted_iota(jnp.int32, (GM_CHUNK, GM_CHUNK), 1)
    causal = row >= col
    z0 = 2 * GM_WIDTH + 3 * ATT_WIDTH
    sz = _silu(proj[:, z0:z0 + D_MODEL])
    for g in range(GM_GROUPS):
        u = _gelu_tanh(proj[:, g * GM_CH:(g + 1) * GM_CH])
        vg = _gelu_tanh(proj[:, GM_WIDTH + g * GM_CH:GM_WIDTH + (g + 1) * GM_CH])
        vn = vg * lax.rsqrt(jnp.mean(vg * vg, axis=-1, keepdims=True) + RMS_EPS) * gmn_ref[g:g + 1, :]
        wsg = jnp.where(causal, ws_ref[g], 0.0).astype(BF16)
        for c in range(tm // GM_CHUNK):
            rows = slice(c * GM_CHUNK, (c + 1) * GM_CHUNK)
            sp = jnp.dot(wsg, vn[rows].astype(BF16), preferred_element_type=F32) + gmb_ref[g]
            a = u[rows] * sp
            ya_ref[0, rows, g * GM_CH:(g + 1) * GM_CH] = (a * sz[rows, g * GM_CH:(g + 1) * GM_CH]).astype(BF16)
    q0 = 2 * GM_WIDTH
    for j in range(ATT_SLABS):
        lanes = slice(j * LANES, (j + 1) * LANES)
        zb_ref[0, j] = sz[:, GM_WIDTH + j * LANES:GM_WIDTH + (j + 1) * LANES]
        q_ref[0, j] = proj[:, q0:q0 + ATT_WIDTH][:, lanes] * (1.0 / math.sqrt(ATT_HD))
        k_ref[0, j] = proj[:, q0 + ATT_WIDTH:q0 + 2 * ATT_WIDTH][:, lanes]
        v_ref[0, j] = proj[:, q0 + 2 * ATT_WIDTH:q0 + 3 * ATT_WIDTH][:, lanes]


def _even_in(x, ln, w_in, gm_norm, gm_ws, gm_b):
    bsz, seq, _ = x.shape
    tm = TOKEN_TILE
    slab = jax.ShapeDtypeStruct((bsz, ATT_SLABS, seq, LANES), F32)
    slab_spec = pl.BlockSpec((1, ATT_SLABS, tm, LANES), lambda b, i: (b, 0, i, 0))
    const = lambda *shape: pl.BlockSpec(shape, lambda b, i: (0,) * len(shape))
    gmb = jnp.broadcast_to(gm_b[:, :, None], (GM_GROUPS, GM_CHUNK, GM_CH))
    return pl.pallas_call(
        _even_in_kernel,
        grid=(bsz, seq // tm),
        in_specs=[
            pl.BlockSpec((1, tm, D_MODEL), lambda b, i: (b, i, 0)),
            const(1, D_MODEL),
            const(D_MODEL, IN_EVEN),
            const(GM_GROUPS, GM_CH),
            const(GM_GROUPS, GM_CHUNK, GM_CHUNK),
            const(GM_GROUPS, GM_CHUNK, GM_CH),
        ],
        out_specs=[
            pl.BlockSpec((1, tm, GM_WIDTH), lambda b, i: (b, i, 0)),
            slab_spec, slab_spec, slab_spec, slab_spec,
        ],
        out_shape=[jax.ShapeDtypeStruct((bsz, seq, GM_WIDTH), BF16), slab, slab, slab, slab],
        compiler_params=_cparams(("parallel", "parallel")),
        name="even_in",
    )(x, ln.reshape(1, D_MODEL), w_in.astype(BF16), gm_norm, gm_ws, gmb)


def _attn_table():
    qi = np.arange(ATT_BLOCK)[:, None]
    kloc = np.arange(2 * ATT_BLOCK)[None, :] - ATT_BLOCK
    j = qi - kloc
    valid = (j >= 0) & (j <= ATT_BLOCK)
    slopes = 2.0 ** (-8.0 * np.arange(1, ATT_HEADS + 1) / ATT_HEADS)
    tabs = []
    for d in DILATIONS:
        bias = -slopes[:, None, None].astype(np.float32) * (j * d).astype(np.float32)[None]
        tabs.append(np.where(valid[None], bias, -np.inf))
    return jnp.asarray(np.stack(tabs), dtype=F32)


def _attn_kernel(q_ref, kp_ref, kc_ref, vp_ref, vc_ref, tab_ref, o_ref,
                 kbuf, vbuf, sbuf, pbuf, acc, mm, mm_sw, ll_sw):
    tile = pl.program_id(2)
    kbuf[0:ATT_TILE, :] = kp_ref[0, 0]
    kbuf[ATT_TILE:2 * ATT_TILE, :] = kc_ref[0, 0]
    vbuf[0:ATT_TILE, :] = vp_ref[0, 0]
    vbuf[ATT_TILE:2 * ATT_TILE, :] = vc_ref[0, 0]
    head0_q = lax.broadcasted_iota(jnp.int32, (ATT_BLOCK, LANES), 1) < ATT_HD
    head0_kv = lax.broadcasted_iota(jnp.int32, (2 * ATT_BLOCK, LANES), 1) < ATT_HD
    prev_cols = lax.broadcasted_iota(jnp.int32, (ATT_BLOCK, 2 * ATT_BLOCK), 1) < ATT_BLOCK
    n_blk = ATT_TILE // ATT_BLOCK

    for p, d in enumerate(DILATIONS):
        shift = int(math.log2(d))

        def rows(i, d=d, shift=shift):
            nl = lax.shift_right_logical(i, shift)
            r = jnp.bitwise_and(i, d - 1)
            qs = nl * (ATT_BLOCK * d) + r
            ks = qs + (ATT_TILE - ATT_BLOCK * d)
            if d == 1:
                return nl, pl.ds(qs, ATT_BLOCK), pl.ds(ks, 2 * ATT_BLOCK)
            return nl, pl.ds(qs, ATT_BLOCK, stride=d), pl.ds(ks, 2 * ATT_BLOCK, stride=d)

        def scores(i, carry, p=p, rows=rows):
            nl, qsl, ksl = rows(i)
            qf = q_ref[0, 0, qsl, :]
            kf = kbuf[ksl, :].astype(BF16)
            pen = jnp.where(jnp.logical_and(tile == 0, nl == 0), -jnp.inf, 0.0).astype(F32)
            first_mask = jnp.where(prev_cols, pen, 0.0)
            qm = jnp.concatenate([jnp.where(head0_q, qf, 0.0), jnp.where(head0_q, 0.0, qf)], axis=0)
            s = lax.dot_general(qm.astype(BF16), kf, (((1,), (1,)), ((), ())), preferred_element_type=F32)
            for hh in range(2):
                sbuf[i, hh] = s[hh * ATT_BLOCK:(hh + 1) * ATT_BLOCK] + (tab_ref[p, hh] + first_mask)
            return carry

        def softmax(i, carry, p=p, rows=rows):
            _, qsl, _ = rows(i)
            m = []
            for hh in range(2):
                s = sbuf[i, hh]
                m.append(jnp.max(s, axis=-1, keepdims=True))
                pbuf[i, hh] = jnp.exp(s - m[hh]).astype(BF16)
            mm[p, qsl, :] = jnp.where(head0_q, m[0], m[1])
            mm_sw[p, qsl, :] = jnp.where(head0_q, m[1], m[0])
            return carry

        def values(i, carry, p=p, rows=rows):
            _, qsl, ksl = rows(i)
            vf = vbuf[ksl, :]
            o0 = jnp.dot(pbuf[i, 0], jnp.where(head0_kv, vf, 1.0).astype(BF16), preferred_element_type=F32)
            o1 = jnp.dot(pbuf[i, 1], jnp.where(head0_kv, 1.0, vf).astype(BF16), preferred_element_type=F32)
            acc[p, qsl, :] = jnp.where(head0_q, o0, o1)
            ll_sw[p, qsl, :] = jnp.where(head0_q, o1, o0)
            return carry

        for stage in (scores, softmax, values):
            lax.fori_loop(0, n_blk, stage, 0, unroll=ATT_UNROLL)

    m_all = jnp.maximum(jnp.maximum(mm[0], mm[1]), mm[2])
    m_all_sw = jnp.maximum(jnp.maximum(mm_sw[0], mm_sw[1]), mm_sw[2])
    num = jnp.zeros((ATT_TILE, LANES), F32)
    den_sw = jnp.zeros((ATT_TILE, LANES), F32)
    for p in range(len(DILATIONS)):
        num = num + jnp.exp(mm[p] - m_all) * acc[p]
        den_sw = den_sw + jnp.exp(mm_sw[p] - m_all_sw) * ll_sw[p]
    o_ref[0, 0] = num / pltpu.roll(den_sw, ATT_HD, 1)


def _attention(q, k, v):
    bsz, _, seq, _ = q.shape
    blk = (1, 1, ATT_TILE, LANES)
    cur = pl.BlockSpec(blk, lambda b, j, t: (b, j, t, 0))
    prev = pl.BlockSpec(blk, lambda b, j, t: (b, j, jnp.maximum(t - 1, 0), 0))
    n_pat = len(DILATIONS)
    return pl.pallas_call(
        _attn_kernel,
        grid=(bsz, ATT_SLABS, seq // ATT_TILE),
        in_specs=[cur, prev, cur, prev, cur,
                  pl.BlockSpec((n_pat, 2, ATT_BLOCK, 2 * ATT_BLOCK), lambda b, j, t: (0, j, 0, 0))],
        out_specs=cur,
        out_shape=jax.ShapeDtypeStruct(q.shape, F32),
        scratch_shapes=[
            pltpu.VMEM((2 * ATT_TILE, LANES), F32),
            pltpu.VMEM((2 * ATT_TILE, LANES), F32),
            pltpu.VMEM((ATT_TILE // ATT_BLOCK, 2, ATT_BLOCK, 2 * ATT_BLOCK), F32),
            pltpu.VMEM((ATT_TILE // ATT_BLOCK, 2, ATT_BLOCK, 2 * ATT_BLOCK), BF16),
            pltpu.VMEM((n_pat, ATT_TILE, LANES), F32),
            pltpu.VMEM((n_pat, ATT_TILE, LANES), F32),
            pltpu.VMEM((n_pat, ATT_TILE, LANES), F32),
            pltpu.VMEM((n_pat, ATT_TILE, LANES), F32),
        ],
        compiler_params=_cparams(("parallel", "parallel", "parallel")),
        name="attention",
    )(q, k, k, v, v, _attn_table())


def _even_out_kernel(x_ref, ya_ref, bo_ref, zb_ref, wout_ref, o_ref):
    parts = [ya_ref[0]] + [(bo_ref[0, j] * zb_ref[0, j]).astype(BF16) for j in range(ATT_SLABS)]
    y = jnp.concatenate(parts, axis=1)
    o_ref[0] = x_ref[0] + jnp.dot(y, wout_ref[...], preferred_element_type=F32)


def _even_out(x, ya, b_out, zb, w_out):
    bsz, seq, _ = x.shape
    tm = TOKEN_TILE
    row = pl.BlockSpec((1, tm, D_MODEL), lambda b, i: (b, i, 0))
    slab_spec = pl.BlockSpec((1, ATT_SLABS, tm, LANES), lambda b, i: (b, 0, i, 0))
    return pl.pallas_call(
        _even_out_kernel,
        grid=(bsz, seq // tm),
        in_specs=[row, pl.BlockSpec((1, tm, GM_WIDTH), lambda b, i: (b, i, 0)), slab_spec, slab_spec,
                  pl.BlockSpec((D_MODEL, D_MODEL), lambda b, i: (0, 0))],
        out_specs=row,
        out_shape=jax.ShapeDtypeStruct(x.shape, F32),
        compiler_params=_cparams(("parallel", "parallel")),
        name="even_out",
    )(x, ya, b_out, zb, w_out.astype(BF16))


def _index_major(n=RW_HEAD, heads=RW_HEADS):
    new = np.arange(n * heads)
    return (new % heads) * n + new // heads


def _head_sum(t):
    s = t[:, 0:LANES]
    for c in range(1, D_MODEL // LANES):
        s = s + t[:, c * LANES:(c + 1) * LANES]
    for sh in (RW_HEADS, 2 * RW_HEADS, 4 * RW_HEADS):
        s = s + pltpu.roll(s, sh, 1)
    return s


def _tile8(s):
    return jnp.concatenate([s] * (D_MODEL // LANES), axis=1)


def _odd_in_kernel(x_ref, halo_ref, ln_ref, mu_ref, wr_ref, wk_ref, wv_ref, wg_ref,
                   w1_ref, w2_ref, a1_ref, a2_ref, w0_ref, a0_ref, kk_ref, ka_ref, rk_ref,
                   r_out, w_out, k_out, v_out, a_out, b_out, g_out, bonus_out):
    tm = x_ref.shape[1]
    ln = ln_ref[...]
    h = _rms(x_ref[0], ln)
    h_halo = _rms(halo_ref[0], ln)
    last = jnp.where(pl.program_id(1) == 0, 0.0, h_halo[7:8, :])
    first_row = lax.broadcasted_iota(jnp.int32, (tm, D_MODEL), 0) == 0
    hprev = jnp.where(first_row, last, pltpu.roll(h, 1, 0))
    xx = hprev - h
    mix = lambda i: (h + xx * mu_ref[i:i + 1, :]).astype(BF16)
    dot = lambda a, b: jnp.dot(a, b, preferred_element_type=F32)
    r = dot(mix(0), wr_ref[...])
    lw = dot(jnp.tanh(dot(mix(1), w1_ref[...])).astype(BF16), w2_ref[...])
    k = dot(mix(2), wk_ref[...])
    v = dot(mix(3), wv_ref[...])
    la = dot(dot(mix(4), a1_ref[...]).astype(BF16), a2_ref[...])
    g = dot(mix(5), wg_ref[...])
    z = -(w0_ref[...] + lw)
    softplus = jnp.maximum(z, 0.0) + jnp.log1p(jnp.exp(-jnp.abs(z)))
    decay = jnp.exp(-jnp.exp(-softplus - 0.5))
    a = jax.nn.sigmoid(a0_ref[...] + la)
    kk = k * kk_ref[...]
    kk = kk * _tile8(lax.rsqrt(jnp.maximum(_head_sum(kk * kk), 1e-24)))
    kmod = k * (1.0 + (a - 1.0) * ka_ref[...])
    bonus = _tile8(_head_sum(r * kmod * rk_ref[...])) * v
    for ref, val in ((r_out, r), (w_out, decay), (k_out, kmod), (v_out, v), (a_out, -kk), (b_out, kk * a)):
        for m in range(SCAN_SLABS):
            ref[0, m] = val[:, m * LANES:(m + 1) * LANES]
    g_out[0] = _silu(g)
    bonus_out[0] = bonus


def _odd_in(x1, ln, mu, wr, wk, wv, wg, w0, w1, w2, a0, a1, a2, k_k, k_a, r_k):
    bsz, seq, _ = x1.shape
    tm = TOKEN_TILE
    perm = _index_major()
    pw = lambda w: w[:, perm].astype(BF16)
    pv = lambda p: p.reshape(-1)[perm].reshape(1, D_MODEL)
    row = pl.BlockSpec((1, tm, D_MODEL), lambda b, i: (b, i, 0))
    halo = pl.BlockSpec((1, 8, D_MODEL), lambda b, i: (b, jnp.maximum(i * (tm // 8) - 1, 0), 0))
    const = lambda *shape: pl.BlockSpec(shape, lambda b, i: (0,) * len(shape))
    mat, vec = const(D_MODEL, D_MODEL), const(1, D_MODEL)
    out = jax.ShapeDtypeStruct(x1.shape, F32)
    slab = jax.ShapeDtypeStruct((bsz, SCAN_SLABS, seq, LANES), F32)
    slab_spec = pl.BlockSpec((1, SCAN_SLABS, tm, LANES), lambda b, i: (b, 0, i, 0))
    return pl.pallas_call(
        _odd_in_kernel,
        grid=(bsz, seq // tm),
        in_specs=[row, halo, vec, const(6, D_MODEL), mat, mat, mat, mat,
                  const(D_MODEL, RW_LORA), const(RW_LORA, D_MODEL),
                  const(D_MODEL, RW_LORA), const(RW_LORA, D_MODEL),
                  vec, vec, vec, vec, vec],
        out_specs=[slab_spec] * 6 + [row] * 2,
        out_shape=[slab] * 6 + [out] * 2,
        compiler_params=_cparams(("parallel", "arbitrary")),
        name="odd_in",
    )(x1, x1, ln.reshape(1, D_MODEL), mu, pw(wr), pw(wk), pw(wv), pw(wg),
      w1.astype(BF16), pw(w2), a1.astype(BF16), pw(a2),
      pv(w0), pv(a0), pv(k_k), pv(k_a), pv(r_k))


def _selection_matrices():
    n_rep = LANES // (2 * RW_HEADS)
    sel_k = np.zeros((2 * LANES, 8 * LANES), np.float32)
    sel_v = np.zeros((2 * LANES, 2 * LANES), np.float32)
    sel_y = np.zeros((2, 2 * LANES, LANES), np.float32)
    for b in range(2):
        for i in range(8):
            for h in range(RW_HEADS):
                src = b * LANES + i * RW_HEADS + h
                for vl in range(n_rep):
                    sel_k[src, i * LANES + vl * 2 * RW_HEADS + b * RW_HEADS + h] = 1.0
                dst = (i // n_rep) * LANES + (i % n_rep) * 2 * RW_HEADS + b * RW_HEADS + h
                sel_v[src, dst] = 1.0
                sel_y[b, dst, i * RW_HEADS + h] = 1.0
    return jnp.asarray(sel_k, BF16), jnp.asarray(sel_v, BF16), jnp.asarray(sel_y, BF16)


def _split_bf16(x):
    hi = x.astype(BF16)
    return hi, (x - hi.astype(F32)).astype(BF16)


def _select(x, sel):
    hi, lo = _split_bf16(x)
    return (jnp.dot(hi, sel, preferred_element_type=F32) + jnp.dot(lo, sel, preferred_element_type=F32))


def _scan_kernel(w0, w1, b0, b1, k0, k1, r0, r1, a0, a1, v0, v1, selk_ref, selv_ref, sely_ref,
                 y_ref, rep, v3, y3, st):
    tc = v0.shape[2]
    n_oct = tc // 8
    OP_W, OP_B, OP_K, OP_R, OP_A = range(5)

    @pl.when(pl.program_id(0) == 0)
    def _():
        st[...] = jnp.zeros(st.shape, F32)

    selk = selk_ref[...]
    for op, (x0, x1) in enumerate(((w0, w1), (b0, b1), (k0, k1), (r0, r1), (a0, a1))):
        for m in range(SCAN_SLABS):
            piece = jnp.concatenate([x0[0, m], x1[0, m]], axis=1)
            rep[op, m] = _select(piece, selk).reshape(n_oct, 8, 8 * LANES)
    selv = selv_ref[...]
    for m in range(SCAN_SLABS):
        out = _select(jnp.concatenate([v0[0, m], v1[0, m]], axis=1), selv)
        v3[:, 2 * m, :] = out[:, 0:LANES]
        v3[:, 2 * m + 1, :] = out[:, LANES:2 * LANES]

    def row(op, k, o, s):
        m, kl = divmod(k, 8)
        return rep[op, m, o, s:s + 1, kl * LANES:(kl + 1) * LANES]

    def dot_state(op, o, s):
        acc = [jnp.zeros((SCAN_VHI, LANES), F32) for _ in range(2)]
        for k in range(RW_HEAD):
            acc[k % 2] = acc[k % 2] + st[k] * row(op, k, o, s)
        return acc[0] + acc[1]

    def octet(o, sa):
        o_next = jnp.minimum(o + 1, n_oct - 1)
        for s in range(8):
            t = o * 8 + s
            nxt = (o, s + 1) if s < 7 else (o_next, 0)
            vt = v3[t]
            accy = [jnp.zeros((SCAN_VHI, LANES), F32) for _ in range(2)]
            accs = [jnp.zeros((SCAN_VHI, LANES), F32) for _ in range(2)]
            for k in range(RW_HEAD):
                snew = st[k] * row(OP_W, k, o, s) + sa * row(OP_B, k, o, s) + vt * row(OP_K, k, o, s)
                st[k] = snew
                accy[k % 2] = accy[k % 2] + snew * row(OP_R, k, o, s)
                accs[k % 2] = accs[k % 2] + snew * row(OP_A, k, *nxt)
            y3[t] = accy[0] + accy[1]
            sa = accs[0] + accs[1]
        return sa

    lax.fori_loop(0, n_oct, octet, dot_state(OP_A, 0, 0))

    for m in range(SCAN_SLABS):
        piece = jnp.concatenate([y3[:, 2 * m, :], y3[:, 2 * m + 1, :]], axis=1)
        hi, lo = _split_bf16(piece)
        for b in range(2):
            sel = sely_ref[b]
            y_ref[b, :, m * LANES:(m + 1) * LANES] = (
                jnp.dot(hi, sel, preferred_element_type=F32) + jnp.dot(lo, sel, preferred_element_type=F32))


def _scan(r, w, k, v, a, b):
    bsz, _, seq, _ = r.shape
    assert bsz == 2
    tc = SCAN_CHUNK
    slab0 = pl.BlockSpec((1, SCAN_SLABS, tc, LANES), lambda i: (0, 0, i, 0))
    slab1 = pl.BlockSpec((1, SCAN_SLABS, tc, LANES), lambda i: (1, 0, i, 0))
    const = lambda *shape: pl.BlockSpec(shape, lambda i: (0,) * len(shape))
    sel_k, sel_v, sel_y = _selection_matrices()
    return pl.pallas_call(
        _scan_kernel,
        grid=(seq // tc,),
        in_specs=[slab0, slab1] * 6 + [const(*sel_k.shape), const(*sel_v.shape), const(*sel_y.shape)],
        out_specs=pl.BlockSpec((bsz, tc, D_MODEL), lambda i: (0, i, 0)),
        out_shape=jax.ShapeDtypeStruct((bsz, seq, D_MODEL), F32),
        scratch_shapes=[
            pltpu.VMEM((5, SCAN_SLABS, tc // 8, 8, 8 * LANES), F32),
            pltpu.VMEM((tc, SCAN_VHI, LANES), F32),
            pltpu.VMEM((tc, SCAN_VHI, LANES), F32),
            pltpu.VMEM((RW_HEAD, SCAN_VHI, LANES), F32),
        ],
        compiler_params=_cparams(("arbitrary",)),
        name="scan",
    )(w, w, b, b, k, k, r, r, a, a, v, v, sel_k, sel_v, sel_y)


def _odd_out_kernel(y_ref, bonus_ref, g_ref, x_ref, lnw_ref, lnb_ref, wo_ref, fn_ref, o_ref):
    y = y_ref[0]
    inv_n = 1.0 / RW_HEAD
    dlt = y - _tile8(_head_sum(y)) * inv_n
    var = _tile8(_head_sum(dlt * dlt)) * inv_n
    yn = dlt * lax.rsqrt(var + LNX_EPS) * lnw_ref[...] + lnb_ref[...]
    z = ((yn + bonus_ref[0]) * g_ref[0]).astype(BF16)
    x2 = x_ref[0] + jnp.dot(z, wo_ref[...], preferred_element_type=F32)
    o_ref[0] = _rms(x2, fn_ref[...])


def _odd_out(y, bonus, gs, x1, lnw, lnb, wo, fn):
    bsz, seq, _ = x1.shape
    tm = TOKEN_TILE
    perm = _index_major()
    pv = lambda p: p.reshape(-1)[perm].reshape(1, D_MODEL)
    row = pl.BlockSpec((1, tm, D_MODEL), lambda b, i: (b, i, 0))
    vec = pl.BlockSpec((1, D_MODEL), lambda b, i: (0, 0))
    return pl.pallas_call(
        _odd_out_kernel,
        grid=(bsz, seq // tm),
        in_specs=[row, row, row, row, vec, vec, pl.BlockSpec((D_MODEL, D_MODEL), lambda b, i: (0, 0)), vec],
        out_specs=row,
        out_shape=jax.ShapeDtypeStruct(x1.shape, F32),
        compiler_params=_cparams(("parallel", "parallel")),
        name="odd_out",
    )(y, bonus, gs, x1, pv(lnw), pv(lnb), wo[perm, :].astype(BF16), fn.reshape(1, D_MODEL))


def kernel(x, ln_even, w_in_even, gm_norm, gm_ws, gm_b, w_out_even, ln_odd, rw_mu, rw_wr, rw_wk,
           rw_wv, rw_wg, rw_w0, rw_w1, rw_w2, rw_a0, rw_a1, rw_a2, rw_kk, rw_ka, rw_rk, rw_lnw,
           rw_lnb, rw_wo, final_norm):
    assert x.shape[1] % ATT_TILE == 0 and x.shape[2] == D_MODEL
    assert ln_even.shape[0] == 1 and ln_odd.shape[0] == 1
    ya, zb, q, k, v = _even_in(x, ln_even[0], w_in_even[0], gm_norm[0], gm_ws[0], gm_b[0])
    b_out = _attention(q, k, v)
    x1 = _even_out(x, ya, b_out, zb, w_out_even[0])
    r, w, kmod, vv, a, b, gs, bonus = _odd_in(
        x1, ln_odd[0], rw_mu[0], rw_wr[0], rw_wk[0], rw_wv[0], rw_wg[0], rw_w0[0], rw_w1[0], rw_w2[0],
        rw_a0[0], rw_a1[0], rw_a2[0], rw_kk[0], rw_ka[0], rw_rk[0])
    y = _scan(r, w, kmod, vv, a, b)
    return _odd_out(y, bonus, gs, x1, rw_lnw[0], rw_lnb[0], rw_wo[0], final_norm)
```

```python
import functools
import math

import numpy as np
import jax
import jax.numpy as jnp
from jax import lax
from jax.experimental import pallas as pl
from jax.experimental.pallas import tpu as pltpu

F32 = jnp.float32
BF16 = jnp.bfloat16

D_MODEL = 1024
GM_GROUPS = 4
GM_CH = 128
GM_CHUNK = 128
GM_WIDTH = GM_GROUPS * GM_CH
ATT_HEADS = 8
ATT_HD = 64
ATT_WIDTH = ATT_HEADS * ATT_HD
ATT_BLOCK = 128
DILATIONS = (1, 4, 16)
ATT_TILE = ATT_BLOCK * 16
ATT_SLABS = ATT_WIDTH // 128
ATT_UNROLL = 4
IN_EVEN = 2 * GM_WIDTH + 3 * ATT_WIDTH + D_MODEL
RW_HEAD = 64
RW_HEADS = D_MODEL // RW_HEAD
RW_LORA = 64
RMS_EPS = 1e-6
LNX_EPS = 64e-5

LANES = 128
TOKEN_TILE = 256
RESID_TILE = 512
SCAN_CHUNK = 128
SCAN_VHI = RW_HEAD // 4
SCAN_SLABS = D_MODEL // LANES
VMEM_LIMIT = 56 * 1024 * 1024


def _cparams(sem):
    return pltpu.CompilerParams(dimension_semantics=sem, vmem_limit_bytes=VMEM_LIMIT)


def _gelu_tanh(x):
    return 0.5 * x * (1.0 + jnp.tanh(0.7978845608028654 * (x + 0.044715 * (x * x * x))))


def _silu(x):
    return x * jax.nn.sigmoid(x)


def _rms(x, g):
    return x * lax.rsqrt(jnp.mean(x * x, axis=-1, keepdims=True) + RMS_EPS) * g


def _even_in_kernel(x_ref, ln_ref, win_ref, gmn_ref, ws_ref, gmb_ref,
                    ya_ref, zb_ref, q_ref, k_ref, v_ref):
    tm = x_ref.shape[1]
    h = _rms(x_ref[0], ln_ref[...])
    proj = jnp.dot(h.astype(BF16), win_ref[...], preferred_element_type=F32)
    row = lax.broadcasted_iota(jnp.int32, (GM_CHUNK, GM_CHUNK), 0)
    col = lax.broadcasted_iota(jnp.int32, (GM_CHUNK, GM_CHUNK), 1)
    causal = row >= col
    z0 = 2 * GM_WIDTH + 3 * ATT_WIDTH
    sz = _silu(proj[:, z0:z0 + D_MODEL])
    for g in range(GM_GROUPS):
        u = _gelu_tanh(proj[:, g * GM_CH:(g + 1) * GM_CH])
        vg = _gelu_tanh(proj[:, GM_WIDTH + g * GM_CH:GM_WIDTH + (g + 1) * GM_CH])
        vn = vg * lax.rsqrt(jnp.mean(vg * vg, axis=-1, keepdims=True) + RMS_EPS) * gmn_ref[g:g + 1, :]
        wsg = jnp.where(causal, ws_ref[g], 0.0).astype(BF16)
        for c in range(tm // GM_CHUNK):
            rows = slice(c * GM_CHUNK, (c + 1) * GM_CHUNK)
            sp = jnp.dot(wsg, vn[rows].astype(BF16), preferred_element_type=F32) + gmb_ref[g]
            a = u[rows] * sp
            ya_ref[0, rows, g * GM_CH:(g + 1) * GM_CH] = (a * sz[rows, g * GM_CH:(g + 1) * GM_CH]).astype(BF16)
    q0 = 2 * GM_WIDTH
    for j in range(ATT_SLABS):
        lanes = slice(j * LANES, (j + 1) * LANES)
        zb_ref[0, j] = sz[:, GM_WIDTH + j * LANES:GM_WIDTH + (j + 1) * LANES]
        q_ref[0, j] = proj[:, q0:q0 + ATT_WIDTH][:, lanes] * (1.0 / math.sqrt(ATT_HD))
        k_ref[0, j] = proj[:, q0 + ATT_WIDTH:q0 + 2 * ATT_WIDTH][:, lanes]
        v_ref[0, j] = proj[:, q0 + 2 * ATT_WIDTH:q0 + 3 * ATT_WIDTH][:, lanes]


def _even_in(x, ln, w_in, gm_norm, gm_ws, gm_b):
    bsz, seq, _ = x.shape
    tm = TOKEN_TILE
    slab = jax.ShapeDtypeStruct((bsz, ATT_SLABS, seq, LANES), F32)
    slab_spec = pl.BlockSpec((1, ATT_SLABS, tm, LANES), lambda b, i: (b, 0, i, 0))
    const = lambda *shape: pl.BlockSpec(shape, lambda b, i: (0,) * len(shape))
    gmb = jnp.broadcast_to(gm_b[:, :, None], (GM_GROUPS, GM_CHUNK, GM_CH))
    return pl.pallas_call(
        _even_in_kernel,
        grid=(bsz, seq // tm),
        in_specs=[
            pl.BlockSpec((1, tm, D_MODEL), lambda b, i: (b, i, 0)),
            const(1, D_MODEL),
            const(D_MODEL, IN_EVEN),
            const(GM_GROUPS, GM_CH),
            const(GM_GROUPS, GM_CHUNK, GM_CHUNK),
            const(GM_GROUPS, GM_CHUNK, GM_CH),
        ],
        out_specs=[
            pl.BlockSpec((1, tm, GM_WIDTH), lambda b, i: (b, i, 0)),
            slab_spec, slab_spec, slab_spec, slab_spec,
        ],
        out_shape=[jax.ShapeDtypeStruct((bsz, seq, GM_WIDTH), BF16), slab, slab, slab, slab],
        compiler_params=_cparams(("parallel", "parallel")),
        name="even_in",
    )(x, ln.reshape(1, D_MODEL), w_in.astype(BF16), gm_norm, gm_ws, gmb)


def _attn_table():
    qi = np.arange(ATT_BLOCK)[:, None]
    kloc = np.arange(2 * ATT_BLOCK)[None, :] - ATT_BLOCK
    j = qi - kloc
    valid = (j >= 0) & (j <= ATT_BLOCK)
    slopes = 2.0 ** (-8.0 * np.arange(1, ATT_HEADS + 1) / ATT_HEADS)
    tabs = []
    for d in DILATIONS:
        bias = -slopes[:, None, None].astype(np.float32) * (j * d).astype(np.float32)[None]
        tabs.append(np.where(valid[None], bias, -np.inf))
    return jnp.asarray(np.stack(tabs), dtype=F32)


def _attn_kernel(q_ref, kp_ref, kc_ref, vp_ref, vc_ref, tab_ref, o_ref,
                 kbuf, vbuf, sbuf, pbuf, acc, mm, mm_sw, ll_sw):
    tile = pl.program_id(2)
    kbuf[0:ATT_TILE, :] = kp_ref[0, 0]
    kbuf[ATT_TILE:2 * ATT_TILE, :] = kc_ref[0, 0]
    vbuf[0:ATT_TILE, :] = vp_ref[0, 0]
    vbuf[ATT_TILE:2 * ATT_TILE, :] = vc_ref[0, 0]
    head0_q = lax.broadcasted_iota(jnp.int32, (ATT_BLOCK, LANES), 1) < ATT_HD
    head0_kv = lax.broadcasted_iota(jnp.int32, (2 * ATT_BLOCK, LANES), 1) < ATT_HD
    prev_cols = lax.broadcasted_iota(jnp.int32, (ATT_BLOCK, 2 * ATT_BLOCK), 1) < ATT_BLOCK
    n_blk = ATT_TILE // ATT_BLOCK

    for p, d in enumerate(DILATIONS):
        shift = int(math.log2(d))

        def rows(i, d=d, shift=shift):
            nl = lax.shift_right_logical(i, shift)
            r = jnp.bitwise_and(i, d - 1)
            qs = nl * (ATT_BLOCK * d) + r
            ks = qs + (ATT_TILE - ATT_BLOCK * d)
            if d == 1:
                return nl, pl.ds(qs, ATT_BLOCK), pl.ds(ks, 2 * ATT_BLOCK)
            return nl, pl.ds(qs, ATT_BLOCK, stride=d), pl.ds(ks, 2 * ATT_BLOCK, stride=d)

        def scores(i, carry, p=p, rows=rows):
            nl, qsl, ksl = rows(i)
            qf = q_ref[0, 0, qsl, :]
            kf = kbuf[ksl, :].astype(BF16)
            pen = jnp.where(jnp.logical_and(tile == 0, nl == 0), -jnp.inf, 0.0).astype(F32)
            first_mask = jnp.where(prev_cols, pen, 0.0)
            qm = jnp.concatenate([jnp.where(head0_q, qf, 0.0), jnp.where(head0_q, 0.0, qf)], axis=0)
            s = lax.dot_general(qm.astype(BF16), kf, (((1,), (1,)), ((), ())), preferred_element_type=F32)
            for hh in range(2):
                sbuf[i, hh] = s[hh * ATT_BLOCK:(hh + 1) * ATT_BLOCK] + (tab_ref[p, hh] + first_mask)
            return carry

        def softmax(i, carry, p=p, rows=rows):
            _, qsl, _ = rows(i)
            m = []
            for hh in range(2):
                s = sbuf[i, hh]
                m.append(jnp.max(s, axis=-1, keepdims=True))
                pbuf[i, hh] = jnp.exp(s - m[hh]).astype(BF16)
            mm[p, qsl, :] = jnp.where(head0_q, m[0], m[1])
            mm_sw[p, qsl, :] = jnp.where(head0_q, m[1], m[0])
            return carry

        def values(i, carry, p=p, rows=rows):
            _, qsl, ksl = rows(i)
            vf = vbuf[ksl, :]
            o0 = jnp.dot(pbuf[i, 0], jnp.where(head0_kv, vf, 1.0).astype(BF16), preferred_element_type=F32)
            o1 = jnp.dot(pbuf[i, 1], jnp.where(head0_kv, 1.0, vf).astype(BF16), preferred_element_type=F32)
            acc[p, qsl, :] = jnp.where(head0_q, o0, o1)
            ll_sw[p, qsl, :] = jnp.where(head0_q, o1, o0)
            return carry

        for stage in (scores, softmax, values):
            lax.fori_loop(0, n_blk, stage, 0, unroll=ATT_UNROLL)

    m_all = jnp.maximum(jnp.maximum(mm[0], mm[1]), mm[2])
    m_all_sw = jnp.maximum(jnp.maximum(mm_sw[0], mm_sw[1]), mm_sw[2])
    num = jnp.zeros((ATT_TILE, LANES), F32)
    den_sw = jnp.zeros((ATT_TILE, LANES), F32)
    for p in range(len(DILATIONS)):
        num = num + jnp.exp(mm[p] - m_all) * acc[p]
        den_sw = den_sw + jnp.exp(mm_sw[p] - m_all_sw) * ll_sw[p]
    o_ref[0, 0] = num / pltpu.roll(den_sw, ATT_HD, 1)


def _attention(q, k, v):
    bsz, _, seq, _ = q.shape
    blk = (1, 1, ATT_TILE, LANES)
    cur = pl.BlockSpec(blk, lambda b, j, t: (b, j, t, 0))
    prev = pl.BlockSpec(blk, lambda b, j, t: (b, j, jnp.maximum(t - 1, 0), 0))
    n_pat = len(DILATIONS)
    return pl.pallas_call(
        _attn_kernel,
        grid=(bsz, ATT_SLABS, seq // ATT_TILE),
        in_specs=[cur, prev, cur, prev, cur,
                  pl.BlockSpec((n_pat, 2, ATT_BLOCK, 2 * ATT_BLOCK), lambda b, j, t: (0, j, 0, 0))],
        out_specs=cur,
        out_shape=jax.ShapeDtypeStruct(q.shape, F32),
        scratch_shapes=[
            pltpu.VMEM((2 * ATT_TILE, LANES), F32),
            pltpu.VMEM((2 * ATT_TILE, LANES), F32),
            pltpu.VMEM((ATT_TILE // ATT_BLOCK, 2, ATT_BLOCK, 2 * ATT_BLOCK), F32),
            pltpu.VMEM((ATT_TILE // ATT_BLOCK, 2, ATT_BLOCK, 2 * ATT_BLOCK), BF16),
            pltpu.VMEM((n_pat, ATT_TILE, LANES), F32),
            pltpu.VMEM((n_pat, ATT_TILE, LANES), F32),
            pltpu.VMEM((n_pat, ATT_TILE, LANES), F32),
            pltpu.VMEM((n_pat, ATT_TILE, LANES), F32),
        ],
        compiler_params=_cparams(("parallel", "parallel", "parallel")),
        name="attention",
    )(q, k, k, v, v, _attn_table())


def _even_out_kernel(x_ref, ya_ref, bo_ref, zb_ref, wout_ref, o_ref):
    parts = [ya_ref[0]] + [(bo_ref[0, j] * zb_ref[0, j]).astype(BF16) for j in range(ATT_SLABS)]
    y = jnp.concatenate(parts, axis=1)
    o_ref[0] = x_ref[0] + jnp.dot(y, wout_ref[...], preferred_element_type=F32)


def _even_out(x, ya, b_out, zb, w_out):
    bsz, seq, _ = x.shape
    tm = RESID_TILE
    row = pl.BlockSpec((1, tm, D_MODEL), lambda b, i: (b, i, 0))
    slab_spec = pl.BlockSpec((1, ATT_SLABS, tm, LANES), lambda b, i: (b, 0, i, 0))
    return pl.pallas_call(
        _even_out_kernel,
        grid=(bsz, seq // tm),
        in_specs=[row, pl.BlockSpec((1, tm, GM_WIDTH), lambda b, i: (b, i, 0)), slab_spec, slab_spec,
                  pl.BlockSpec((D_MODEL, D_MODEL), lambda b, i: (0, 0))],
        out_specs=row,
        out_shape=jax.ShapeDtypeStruct(x.shape, F32),
        compiler_params=_cparams(("parallel", "parallel")),
        name="even_out",
    )(x, ya, b_out, zb, w_out.astype(BF16))


def _index_major(n=RW_HEAD, heads=RW_HEADS):
    new = np.arange(n * heads)
    return (new % heads) * n + new // heads


def _head_sum(t):
    s = t[:, 0:LANES]
    for c in range(1, D_MODEL // LANES):
        s = s + t[:, c * LANES:(c + 1) * LANES]
    for sh in (RW_HEADS, 2 * RW_HEADS, 4 * RW_HEADS):
        s = s + pltpu.roll(s, sh, 1)
    return s


def _tile8(s):
    return jnp.concatenate([s] * (D_MODEL // LANES), axis=1)


def _odd_in_kernel(x_ref, halo_ref, ln_ref, mu_ref, wr_ref, wk_ref, wv_ref, wg_ref,
                   w1_ref, w2_ref, a1_ref, a2_ref, w0_ref, a0_ref, kk_ref, ka_ref, rk_ref,
                   r_out, w_out, k_out, v_out, a_out, b_out, g_out, bonus_out):
    tm = x_ref.shape[1]
    ln = ln_ref[...]
    h = _rms(x_ref[0], ln)
    h_halo = _rms(halo_ref[0], ln)
    last = jnp.where(pl.program_id(1) == 0, 0.0, h_halo[7:8, :])
    first_row = lax.broadcasted_iota(jnp.int32, (tm, D_MODEL), 0) == 0
    hprev = jnp.where(first_row, last, pltpu.roll(h, 1, 0))
    xx = hprev - h
    mix = lambda i: (h + xx * mu_ref[i:i + 1, :]).astype(BF16)
    dot = lambda a, b: jnp.dot(a, b, preferred_element_type=F32)
    r = dot(mix(0), wr_ref[...])
    lw = dot(jnp.tanh(dot(mix(1), w1_ref[...])).astype(BF16), w2_ref[...])
    k = dot(mix(2), wk_ref[...])
    v = dot(mix(3), wv_ref[...])
    la = dot(dot(mix(4), a1_ref[...]).astype(BF16), a2_ref[...])
    g = dot(mix(5), wg_ref[...])
    z = -(w0_ref[...] + lw)
    softplus = jnp.maximum(z, 0.0) + jnp.log1p(jnp.exp(-jnp.abs(z)))
    decay = jnp.exp(-jnp.exp(-softplus - 0.5))
    a = jax.nn.sigmoid(a0_ref[...] + la)
    kk = k * kk_ref[...]
    kk = kk * _tile8(lax.rsqrt(jnp.maximum(_head_sum(kk * kk), 1e-24)))
    kmod = k * (1.0 + (a - 1.0) * ka_ref[...])
    bonus = _tile8(_head_sum(r * kmod * rk_ref[...])) * v
    for ref, val in ((r_out, r), (w_out, decay), (k_out, kmod), (v_out, v), (a_out, -kk), (b_out, kk * a)):
        for m in range(SCAN_SLABS):
            ref[0, m] = val[:, m * LANES:(m + 1) * LANES]
    g_out[0] = _silu(g)
    bonus_out[0] = bonus


def _odd_in(x1, ln, mu, wr, wk, wv, wg, w0, w1, w2, a0, a1, a2, k_k, k_a, r_k):
    bsz, seq, _ = x1.shape
    tm = TOKEN_TILE
    perm = _index_major()
    pw = lambda w: w[:, perm].astype(BF16)
    pv = lambda p: p.reshape(-1)[perm].reshape(1, D_MODEL)
    row = pl.BlockSpec((1, tm, D_MODEL), lambda b, i: (b, i, 0))
    halo = pl.BlockSpec((1, 8, D_MODEL), lambda b, i: (b, jnp.maximum(i * (tm // 8) - 1, 0), 0))
    const = lambda *shape: pl.BlockSpec(shape, lambda b, i: (0,) * len(shape))
    mat, vec = const(D_MODEL, D_MODEL), const(1, D_MODEL)
    out = jax.ShapeDtypeStruct(x1.shape, F32)
    slab = jax.ShapeDtypeStruct((bsz, SCAN_SLABS, seq, LANES), F32)
    slab_spec = pl.BlockSpec((1, SCAN_SLABS, tm, LANES), lambda b, i: (b, 0, i, 0))
    return pl.pallas_call(
        _odd_in_kernel,
        grid=(bsz, seq // tm),
        in_specs=[row, halo, vec, const(6, D_MODEL), mat, mat, mat, mat,
                  const(D_MODEL, RW_LORA), const(RW_LORA, D_MODEL),
                  const(D_MODEL, RW_LORA), const(RW_LORA, D_MODEL),
                  vec, vec, vec, vec, vec],
        out_specs=[slab_spec] * 6 + [row] * 2,
        out_shape=[slab] * 6 + [out] * 2,
        compiler_params=_cparams(("parallel", "arbitrary")),
        name="odd_in",
    )(x1, x1, ln.reshape(1, D_MODEL), mu, pw(wr), pw(wk), pw(wv), pw(wg),
      w1.astype(BF16), pw(w2), a1.astype(BF16), pw(a2),
      pv(w0), pv(a0), pv(k_k), pv(k_a), pv(r_k))


def _selection_matrices():
    n_rep = LANES // (2 * RW_HEADS)
    sel_k = np.zeros((2 * LANES, 8 * LANES), np.float32)
    sel_v = np.zeros((2 * LANES, 2 * LANES), np.float32)
    sel_y = np.zeros((2 * LANES, 2 * LANES), np.float32)
    for b in range(2):
        for i in range(8):
            for h in range(RW_HEADS):
                src = b * LANES + i * RW_HEADS + h
                for vl in range(n_rep):
                    sel_k[src, i * LANES + vl * 2 * RW_HEADS + b * RW_HEADS + h] = 1.0
                dst = (i // n_rep) * LANES + (i % n_rep) * 2 * RW_HEADS + b * RW_HEADS + h
                sel_v[src, dst] = 1.0
                sel_y[dst, src] = 1.0
    return jnp.asarray(sel_k, BF16), jnp.asarray(sel_v, BF16), jnp.asarray(sel_y, BF16)


def _split_bf16(x):
    hi = x.astype(BF16)
    return hi, (x - hi.astype(F32)).astype(BF16)


def _select(x, sel):
    hi, lo = _split_bf16(x)
    return (jnp.dot(hi, sel, preferred_element_type=F32) + jnp.dot(lo, sel, preferred_element_type=F32))


def _scan_kernel(w0, w1, b0, b1, k0, k1, r0, r1, a0, a1, v0, v1, selk_ref, selv_ref, sely_ref,
                 y_ref, rep, v3, y3, st):
    tc = v0.shape[2]
    n_oct = tc // 8
    OP_W, OP_B, OP_K, OP_R, OP_A = range(5)

    @pl.when(pl.program_id(0) == 0)
    def _():
        st[...] = jnp.zeros(st.shape, F32)

    selk = selk_ref[...]
    for op, (x0, x1) in enumerate(((w0, w1), (b0, b1), (k0, k1), (r0, r1), (a0, a1))):
        for m in range(SCAN_SLABS):
            piece = jnp.concatenate([x0[0, m], x1[0, m]], axis=1)
            if op in (OP_K, OP_R):
                laid = jnp.dot(piece.astype(BF16), selk, preferred_element_type=F32)
            else:
                laid = _select(piece, selk)
            rep[op, m] = laid.reshape(n_oct, 8, 8 * LANES)
    selv = selv_ref[...]
    for m in range(SCAN_SLABS):
        out = _select(jnp.concatenate([v0[0, m], v1[0, m]], axis=1), selv)
        v3[:, 2 * m, :] = out[:, 0:LANES]
        v3[:, 2 * m + 1, :] = out[:, LANES:2 * LANES]

    def row(op, k, o, s):
        m, kl = divmod(k, 8)
        return rep[op, m, o, s:s + 1, kl * LANES:(kl + 1) * LANES]

    def dot_state(op, o, s):
        acc = [jnp.zeros((SCAN_VHI, LANES), F32) for _ in range(2)]
        for k in range(RW_HEAD):
            acc[k % 2] = acc[k % 2] + st[k] * row(op, k, o, s)
        return acc[0] + acc[1]

    def octet(o, sa):
        o_next = jnp.minimum(o + 1, n_oct - 1)
        for s in range(8):
            t = o * 8 + s
            nxt = (o, s + 1) if s < 7 else (o_next, 0)
            vt = v3[t]
            accy = [jnp.zeros((SCAN_VHI, LANES), F32) for _ in range(2)]
            accs = [jnp.zeros((SCAN_VHI, LANES), F32) for _ in range(2)]
            for k in range(RW_HEAD):
                snew = st[k] * row(OP_W, k, o, s) + sa * row(OP_B, k, o, s) + vt * row(OP_K, k, o, s)
                st[k] = snew
                accy[k % 2] = accy[k % 2] + snew * row(OP_R, k, o, s)
                accs[k % 2] = accs[k % 2] + snew * row(OP_A, k, *nxt)
            y3[t] = accy[0] + accy[1]
            sa = accs[0] + accs[1]
        return sa

    lax.fori_loop(0, n_oct, octet, dot_state(OP_A, 0, 0))

    sely = sely_ref[...]
    for m in range(SCAN_SLABS):
        piece = jnp.concatenate([y3[:, 2 * m, :], y3[:, 2 * m + 1, :]], axis=1)
        both = _select(piece, sely)
        for b in range(2):
            y_ref[b, :, m * LANES:(m + 1) * LANES] = both[:, b * LANES:(b + 1) * LANES]


def _scan(r, w, k, v, a, b):
    bsz, _, seq, _ = r.shape
    assert bsz == 2
    tc = SCAN_CHUNK
    slab0 = pl.BlockSpec((1, SCAN_SLABS, tc, LANES), lambda i: (0, 0, i, 0))
    slab1 = pl.BlockSpec((1, SCAN_SLABS, tc, LANES), lambda i: (1, 0, i, 0))
    const = lambda *shape: pl.BlockSpec(shape, lambda i: (0,) * len(shape))
    sel_k, sel_v, sel_y = _selection_matrices()
    return pl.pallas_call(
        _scan_kernel,
        grid=(seq // tc,),
        in_specs=[slab0, slab1] * 6 + [const(*sel_k.shape), const(*sel_v.shape), const(*sel_y.shape)],
        out_specs=pl.BlockSpec((bsz, tc, D_MODEL), lambda i: (0, i, 0)),
        out_shape=jax.ShapeDtypeStruct((bsz, seq, D_MODEL), F32),
        scratch_shapes=[
            pltpu.VMEM((5, SCAN_SLABS, tc // 8, 8, 8 * LANES), F32),
            pltpu.VMEM((tc, SCAN_VHI, LANES), F32),
            pltpu.VMEM((tc, SCAN_VHI, LANES), F32),
            pltpu.VMEM((RW_HEAD, SCAN_VHI, LANES), F32),
        ],
        compiler_params=_cparams(("arbitrary",)),
        name="scan",
    )(w, w, b, b, k, k, r, r, a, a, v, v, sel_k, sel_v, sel_y)


def _odd_out_kernel(y_ref, bonus_ref, g_ref, x_ref, lnw_ref, lnb_ref, wo_ref, fn_ref, o_ref):
    y = y_ref[0]
    inv_n = 1.0 / RW_HEAD
    dlt = y - _tile8(_head_sum(y)) * inv_n
    var = _tile8(_head_sum(dlt * dlt)) * inv_n
    yn = dlt * lax.rsqrt(var + LNX_EPS) * lnw_ref[...] + lnb_ref[...]
    z = ((yn + bonus_ref[0]) * g_ref[0]).astype(BF16)
    x2 = x_ref[0] + jnp.dot(z, wo_ref[...], preferred_element_type=F32)
    o_ref[0] = _rms(x2, fn_ref[...])


def _odd_out(y, bonus, gs, x1, lnw, lnb, wo, fn):
    bsz, seq, _ = x1.shape
    tm = RESID_TILE
    perm = _index_major()
    pv = lambda p: p.reshape(-1)[perm].reshape(1, D_MODEL)
    row = pl.BlockSpec((1, tm, D_MODEL), lambda b, i: (b, i, 0))
    vec = pl.BlockSpec((1, D_MODEL), lambda b, i: (0, 0))
    return pl.pallas_call(
        _odd_out_kernel,
        grid=(bsz, seq // tm),
        in_specs=[row, row, row, row, vec, vec, pl.BlockSpec((D_MODEL, D_MODEL), lambda b, i: (0, 0)), vec],
        out_specs=row,
        out_shape=jax.ShapeDtypeStruct(x1.shape, F32),
        compiler_params=_cparams(("parallel", "parallel")),
        name="odd_out",
    )(y, bonus, gs, x1, pv(lnw), pv(lnb), wo[perm, :].astype(BF16), fn.reshape(1, D_MODEL))


def kernel(x, ln_even, w_in_even, gm_norm, gm_ws, gm_b, w_out_even, ln_odd, rw_mu, rw_wr, rw_wk,
           rw_wv, rw_wg, rw_w0, rw_w1, rw_w2, rw_a0, rw_a1, rw_a2, rw_kk, rw_ka, rw_rk, rw_lnw,
           rw_lnb, rw_wo, final_norm):
    assert x.shape[1] % ATT_TILE == 0 and x.shape[2] == D_MODEL
    assert ln_even.shape[0] == 1 and ln_odd.shape[0] == 1
    ya, zb, q, k, v = _even_in(x, ln_even[0], w_in_even[0], gm_norm[0], gm_ws[0], gm_b[0])
    b_out = _attention(q, k, v)
    x1 = _even_out(x, ya, b_out, zb, w_out_even[0])
    r, w, kmod, vv, a, b, gs, bonus = _odd_in(
        x1, ln_odd[0], rw_mu[0], rw_wr[0], rw_wk[0], rw_wv[0], rw_wg[0], rw_w0[0], rw_w1[0], rw_w2[0],
        rw_a0[0], rw_a1[0], rw_a2[0], rw_kk[0], rw_ka[0], rw_rk[0])
    y = _scan(r, w, kmod, vv, a, b)
    return _odd_out(y, bonus, gs, x1, rw_lnw[0], rw_lnb[0], rw_wo[0], final_norm)
```

```python
import functools
import math

import numpy as np
import jax
import jax.numpy as jnp
from jax import lax
from jax.experimental import pallas as pl
from jax.experimental.pallas import tpu as pltpu

F32 = jnp.float32
BF16 = jnp.bfloat16

D_MODEL = 1024
GM_GROUPS = 4
GM_CH = 128
GM_CHUNK = 128
GM_WIDTH = GM_GROUPS * GM_CH
ATT_HEADS = 8
ATT_HD = 64
ATT_WIDTH = ATT_HEADS * ATT_HD
ATT_BLOCK = 128
DILATIONS = (1, 4, 16)
ATT_TILE = ATT_BLOCK * 16
ATT_SLABS = ATT_WIDTH // 128
ATT_UNROLL = 16
IN_EVEN = 2 * GM_WIDTH + 3 * ATT_WIDTH + D_MODEL
RW_HEAD = 64
RW_HEADS = D_MODEL // RW_HEAD
RW_LORA = 64
RMS_EPS = 1e-6
LNX_EPS = 64e-5

LANES = 128
TOKEN_TILE = 256
RESID_TILE = 512
SCAN_CHUNK = 128
SCAN_VHI = RW_HEAD // 4
SCAN_SLABS = D_MODEL // LANES
VMEM_LIMIT = 56 * 1024 * 1024


def _cparams(sem):
    return pltpu.CompilerParams(dimension_semantics=sem, vmem_limit_bytes=VMEM_LIMIT)


def _gelu_tanh(x):
    return 0.5 * x * (1.0 + jnp.tanh(0.7978845608028654 * (x + 0.044715 * (x * x * x))))


def _silu(x):
    return x * jax.nn.sigmoid(x)


def _rms(x, g):
    return x * lax.rsqrt(jnp.mean(x * x, axis=-1, keepdims=True) + RMS_EPS) * g


def _even_in_kernel(x_ref, ln_ref, win_ref, gmn_ref, ws_ref, gmb_ref,
                    ya_ref, zb_ref, q_ref, k_ref, v_ref):
    tm = x_ref.shape[1]
    h = _rms(x_ref[0], ln_ref[...])
    proj = jnp.dot(h.astype(BF16), win_ref[...], preferred_element_type=F32)
    row = lax.broadcasted_iota(jnp.int32, (GM_CHUNK, GM_CHUNK), 0)
    col = lax.broadcasted_iota(jnp.int32, (GM_CHUNK, GM_CHUNK), 1)
    causal = row >= col
    z0 = 2 * GM_WIDTH + 3 * ATT_WIDTH
    sz = _silu(proj[:, z0:z0 + D_MODEL])
    for g in range(GM_GROUPS):
        u = _gelu_tanh(proj[:, g * GM_CH:(g + 1) * GM_CH])
        vg = _gelu_tanh(proj[:, GM_WIDTH + g * GM_CH:GM_WIDTH + (g + 1) * GM_CH])
        vn = vg * lax.rsqrt(jnp.mean(vg * vg, axis=-1, keepdims=True) + RMS_EPS) * gmn_ref[g:g + 1, :]
        wsg = jnp.where(causal, ws_ref[g], 0.0).astype(BF16)
        for c in range(tm // GM_CHUNK):
            rows = slice(c * GM_CHUNK, (c + 1) * GM_CHUNK)
            sp = jnp.dot(wsg, vn[rows].astype(BF16), preferred_element_type=F32) + gmb_ref[g]
            a = u[rows] * sp
            ya_ref[0, rows, g * GM_CH:(g + 1) * GM_CH] = (a * sz[rows, g * GM_CH:(g + 1) * GM_CH]).astype(BF16)
    q0 = 2 * GM_WIDTH
    for j in range(ATT_SLABS):
        lanes = slice(j * LANES, (j + 1) * LANES)
        zb_ref[0, j] = sz[:, GM_WIDTH + j * LANES:GM_WIDTH + (j + 1) * LANES]
        q_ref[0, j] = proj[:, q0:q0 + ATT_WIDTH][:, lanes] * (1.0 / math.sqrt(ATT_HD))
        k_ref[0, j] = proj[:, q0 + ATT_WIDTH:q0 + 2 * ATT_WIDTH][:, lanes]
        v_ref[0, j] = proj[:, q0 + 2 * ATT_WIDTH:q0 + 3 * ATT_WIDTH][:, lanes]


def _even_in(x, ln, w_in, gm_norm, gm_ws, gm_b):
    bsz, seq, _ = x.shape
    tm = TOKEN_TILE
    slab = jax.ShapeDtypeStruct((bsz, ATT_SLABS, seq, LANES), F32)
    slab_spec = pl.BlockSpec((1, ATT_SLABS, tm, LANES), lambda b, i: (b, 0, i, 0))
    const = lambda *shape: pl.BlockSpec(shape, lambda b, i: (0,) * len(shape))
    gmb = jnp.broadcast_to(gm_b[:, :, None], (GM_GROUPS, GM_CHUNK, GM_CH))
    return pl.pallas_call(
        _even_in_kernel,
        grid=(bsz, seq // tm),
        in_specs=[
            pl.BlockSpec((1, tm, D_MODEL), lambda b, i: (b, i, 0)),
            const(1, D_MODEL),
            const(D_MODEL, IN_EVEN),
            const(GM_GROUPS, GM_CH),
            const(GM_GROUPS, GM_CHUNK, GM_CHUNK),
            const(GM_GROUPS, GM_CHUNK, GM_CH),
        ],
        out_specs=[
            pl.BlockSpec((1, tm, GM_WIDTH), lambda b, i: (b, i, 0)),
            slab_spec, slab_spec, slab_spec, slab_spec,
        ],
        out_shape=[jax.ShapeDtypeStruct((bsz, seq, GM_WIDTH), BF16), slab, slab, slab, slab],
        compiler_params=_cparams(("parallel", "parallel")),
        name="even_in",
    )(x, ln.reshape(1, D_MODEL), w_in.astype(BF16), gm_norm, gm_ws, gmb)


def _attn_table():
    qi = np.arange(ATT_BLOCK)[:, None]
    kloc = np.arange(2 * ATT_BLOCK)[None, :] - ATT_BLOCK
    j = qi - kloc
    valid = (j >= 0) & (j <= ATT_BLOCK)
    slopes = 2.0 ** (-8.0 * np.arange(1, ATT_HEADS + 1) / ATT_HEADS)
    tabs = []
    for d in DILATIONS:
        bias = -slopes[:, None, None].astype(np.float32) * (j * d).astype(np.float32)[None]
        later = np.where(valid[None], bias, -np.inf)
        first = np.where(kloc[None] >= 0, later, -np.inf)
        tabs.append(np.stack([later, first], axis=1))
    return jnp.asarray(np.stack(tabs), dtype=F32)


def _attn_kernel(q_ref, kp_ref, kc_ref, vp_ref, vc_ref, tab_ref, o_ref,
                 kbuf, vbuf, sbuf, pbuf, acc, mm, mm_sw, ll_sw):
    tile = pl.program_id(2)
    kbuf[0:ATT_TILE, :] = kp_ref[0, 0]
    kbuf[ATT_TILE:2 * ATT_TILE, :] = kc_ref[0, 0]
    vbuf[0:ATT_TILE, :] = vp_ref[0, 0]
    vbuf[ATT_TILE:2 * ATT_TILE, :] = vc_ref[0, 0]
    head0_q = lax.broadcasted_iota(jnp.int32, (ATT_BLOCK, LANES), 1) < ATT_HD
    head0_kv = lax.broadcasted_iota(jnp.int32, (2 * ATT_BLOCK, LANES), 1) < ATT_HD
    n_blk = ATT_TILE // ATT_BLOCK

    for p, d in enumerate(DILATIONS):
        shift = int(math.log2(d))

        def rows(i, d=d, shift=shift):
            nl = lax.shift_right_logical(i, shift)
            r = jnp.bitwise_and(i, d - 1)
            qs = nl * (ATT_BLOCK * d) + r
            ks = qs + (ATT_TILE - ATT_BLOCK * d)
            if d == 1:
                return nl, pl.ds(qs, ATT_BLOCK), pl.ds(ks, 2 * ATT_BLOCK)
            return nl, pl.ds(qs, ATT_BLOCK, stride=d), pl.ds(ks, 2 * ATT_BLOCK, stride=d)

        def scores(i, carry, p=p, rows=rows):
            nl, qsl, ksl = rows(i)
            qf = q_ref[0, 0, qsl, :]
            kf = kbuf[ksl, :].astype(BF16)
            first = jnp.logical_and(tile == 0, nl == 0).astype(jnp.int32)
            qm = jnp.concatenate([jnp.where(head0_q, qf, 0.0), jnp.where(head0_q, 0.0, qf)], axis=0)
            s = lax.dot_general(qm.astype(BF16), kf, (((1,), (1,)), ((), ())), preferred_element_type=F32)
            for hh in range(2):
                sbuf[i, hh] = s[hh * ATT_BLOCK:(hh + 1) * ATT_BLOCK] + tab_ref[p, hh, first]
            return carry

        def softmax(i, carry, p=p, rows=rows):
            _, qsl, _ = rows(i)
            m = []
            for hh in range(2):
                s = sbuf[i, hh]
                m.append(jnp.max(s, axis=-1, keepdims=True))
                pbuf[i, hh] = jnp.exp(s - m[hh]).astype(BF16)
            mm[p, qsl, :] = jnp.where(head0_q, m[0], m[1])
            mm_sw[p, qsl, :] = jnp.where(head0_q, m[1], m[0])
            return carry

        def values(i, carry, p=p, rows=rows):
            _, qsl, ksl = rows(i)
            vf = vbuf[ksl, :]
            o0 = jnp.dot(pbuf[i, 0], jnp.where(head0_kv, vf, 1.0).astype(BF16), preferred_element_type=F32)
            o1 = jnp.dot(pbuf[i, 1], jnp.where(head0_kv, 1.0, vf).astype(BF16), preferred_element_type=F32)
            acc[p, qsl, :] = jnp.where(head0_q, o0, o1)
            ll_sw[p, qsl, :] = jnp.where(head0_q, o1, o0)
            return carry

        for stage in (scores, softmax, values):
            lax.fori_loop(0, n_blk, stage, 0, unroll=ATT_UNROLL)

    m_all = jnp.maximum(jnp.maximum(mm[0], mm[1]), mm[2])
    m_all_sw = jnp.maximum(jnp.maximum(mm_sw[0], mm_sw[1]), mm_sw[2])
    num = jnp.zeros((ATT_TILE, LANES), F32)
    den_sw = jnp.zeros((ATT_TILE, LANES), F32)
    for p in range(len(DILATIONS)):
        num = num + jnp.exp(mm[p] - m_all) * acc[p]
        den_sw = den_sw + jnp.exp(mm_sw[p] - m_all_sw) * ll_sw[p]
    o_ref[0, 0] = num / pltpu.roll(den_sw, ATT_HD, 1)


def _attention(q, k, v):
    bsz, _, seq, _ = q.shape
    blk = (1, 1, ATT_TILE, LANES)
    cur = pl.BlockSpec(blk, lambda b, j, t: (b, j, t, 0))
    prev = pl.BlockSpec(blk, lambda b, j, t: (b, j, jnp.maximum(t - 1, 0), 0))
    n_pat = len(DILATIONS)
    return pl.pallas_call(
        _attn_kernel,
        grid=(bsz, ATT_SLABS, seq // ATT_TILE),
        in_specs=[cur, prev, cur, prev, cur,
                  pl.BlockSpec((n_pat, 2, 2, ATT_BLOCK, 2 * ATT_BLOCK), lambda b, j, t: (0, j, 0, 0, 0))],
        out_specs=cur,
        out_shape=jax.ShapeDtypeStruct(q.shape, F32),
        scratch_shapes=[
            pltpu.VMEM((2 * ATT_TILE, LANES), F32),
            pltpu.VMEM((2 * ATT_TILE, LANES), F32),
            pltpu.VMEM((ATT_TILE // ATT_BLOCK, 2, ATT_BLOCK, 2 * ATT_BLOCK), F32),
            pltpu.VMEM((ATT_TILE // ATT_BLOCK, 2, ATT_BLOCK, 2 * ATT_BLOCK), BF16),
            pltpu.VMEM((n_pat, ATT_TILE, LANES), F32),
            pltpu.VMEM((n_pat, ATT_TILE, LANES), F32),
            pltpu.VMEM((n_pat, ATT_TILE, LANES), F32),
            pltpu.VMEM((n_pat, ATT_TILE, LANES), F32),
        ],
        compiler_params=_cparams(("parallel", "parallel", "parallel")),
        name="attention",
    )(q, k, k, v, v, _attn_table())


def _even_out_kernel(x_ref, ya_ref, bo_ref, zb_ref, wout_ref, o_ref):
    parts = [ya_ref[0]] + [(bo_ref[0, j] * zb_ref[0, j]).astype(BF16) for j in range(ATT_SLABS)]
    y = jnp.concatenate(parts, axis=1)
    o_ref[0] = x_ref[0] + jnp.dot(y, wout_ref[...], preferred_element_type=F32)


def _even_out(x, ya, b_out, zb, w_out):
    bsz, seq, _ = x.shape
    tm = RESID_TILE
    row = pl.BlockSpec((1, tm, D_MODEL), lambda b, i: (b, i, 0))
    slab_spec = pl.BlockSpec((1, ATT_SLABS, tm, LANES), lambda b, i: (b, 0, i, 0))
    return pl.pallas_call(
        _even_out_kernel,
        grid=(bsz, seq // tm),
        in_specs=[row, pl.BlockSpec((1, tm, GM_WIDTH), lambda b, i: (b, i, 0)), slab_spec, slab_spec,
                  pl.BlockSpec((D_MODEL, D_MODEL), lambda b, i: (0, 0))],
        out_specs=row,
        out_shape=jax.ShapeDtypeStruct(x.shape, F32),
        compiler_params=_cparams(("parallel", "parallel")),
        name="even_out",
    )(x, ya, b_out, zb, w_out.astype(BF16))


def _index_major(n=RW_HEAD, heads=RW_HEADS):
    new = np.arange(n * heads)
    return (new % heads) * n + new // heads


def _head_sum(t):
    s = t[:, 0:LANES]
    for c in range(1, D_MODEL // LANES):
        s = s + t[:, c * LANES:(c + 1) * LANES]
    for sh in (RW_HEADS, 2 * RW_HEADS, 4 * RW_HEADS):
        s = s + pltpu.roll(s, sh, 1)
    return s


def _tile8(s):
    return jnp.concatenate([s] * (D_MODEL // LANES), axis=1)


def _odd_in_kernel(x_ref, halo_ref, ln_ref, mu_ref, wr_ref, wk_ref, wv_ref, wg_ref,
                   w1_ref, w2_ref, a1_ref, a2_ref, w0_ref, a0_ref, kk_ref, ka_ref, rk_ref,
                   r_out, w_out, k_out, v_out, a_out, b_out, g_out, bonus_out):
    tm = x_ref.shape[1]
    ln = ln_ref[...]
    h = _rms(x_ref[0], ln)
    h_halo = _rms(halo_ref[0], ln)
    last = jnp.where(pl.program_id(1) == 0, 0.0, h_halo[7:8, :])
    first_row = lax.broadcasted_iota(jnp.int32, (tm, D_MODEL), 0) == 0
    hprev = jnp.where(first_row, last, pltpu.roll(h, 1, 0))
    xx = hprev - h
    mix = lambda i: (h + xx * mu_ref[i:i + 1, :]).astype(BF16)
    dot = lambda a, b: jnp.dot(a, b, preferred_element_type=F32)
    r = dot(mix(0), wr_ref[...])
    lw = dot(jnp.tanh(dot(mix(1), w1_ref[...])).astype(BF16), w2_ref[...])
    k = dot(mix(2), wk_ref[...])
    v = dot(mix(3), wv_ref[...])
    la = dot(dot(mix(4), a1_ref[...]).astype(BF16), a2_ref[...])
    g = dot(mix(5), wg_ref[...])
    z = -(w0_ref[...] + lw)
    softplus = jnp.maximum(z, 0.0) + jnp.log1p(jnp.exp(-jnp.abs(z)))
    decay = jnp.exp(-jnp.exp(-softplus - 0.5))
    a = jax.nn.sigmoid(a0_ref[...] + la)
    kk = k * kk_ref[...]
    kk = kk * _tile8(lax.rsqrt(jnp.maximum(_head_sum(kk * kk), 1e-24)))
    kmod = k * (1.0 + (a - 1.0) * ka_ref[...])
    bonus = _tile8(_head_sum(r * kmod * rk_ref[...])) * v
    for ref, val in ((r_out, r), (w_out, decay), (k_out, kmod), (v_out, v), (a_out, -kk), (b_out, kk * a)):
        for m in range(SCAN_SLABS):
            ref[0, m] = val[:, m * LANES:(m + 1) * LANES]
    g_out[0] = _silu(g)
    bonus_out[0] = bonus


def _odd_in(x1, ln, mu, wr, wk, wv, wg, w0, w1, w2, a0, a1, a2, k_k, k_a, r_k):
    bsz, seq, _ = x1.shape
    tm = TOKEN_TILE
    perm = _index_major()
    pw = lambda w: w[:, perm].astype(BF16)
    pv = lambda p: p.reshape(-1)[perm].reshape(1, D_MODEL)
    row = pl.BlockSpec((1, tm, D_MODEL), lambda b, i: (b, i, 0))
    halo = pl.BlockSpec((1, 8, D_MODEL), lambda b, i: (b, jnp.maximum(i * (tm // 8) - 1, 0), 0))
    const = lambda *shape: pl.BlockSpec(shape, lambda b, i: (0,) * len(shape))
    mat, vec = const(D_MODEL, D_MODEL), const(1, D_MODEL)
    out = jax.ShapeDtypeStruct(x1.shape, F32)
    slab = jax.ShapeDtypeStruct((bsz, SCAN_SLABS, seq, LANES), F32)
    slab_spec = pl.BlockSpec((1, SCAN_SLABS, tm, LANES), lambda b, i: (b, 0, i, 0))
    return pl.pallas_call(
        _odd_in_kernel,
        grid=(bsz, seq // tm),
        in_specs=[row, halo, vec, const(6, D_MODEL), mat, mat, mat, mat,
                  const(D_MODEL, RW_LORA), const(RW_LORA, D_MODEL),
                  const(D_MODEL, RW_LORA), const(RW_LORA, D_MODEL),
                  vec, vec, vec, vec, vec],
        out_specs=[slab_spec] * 6 + [row] * 2,
        out_shape=[slab] * 6 + [out] * 2,
        compiler_params=_cparams(("parallel", "arbitrary")),
        name="odd_in",
    )(x1, x1, ln.reshape(1, D_MODEL), mu, pw(wr), pw(wk), pw(wv), pw(wg),
      w1.astype(BF16), pw(w2), a1.astype(BF16), pw(a2),
      pv(w0), pv(a0), pv(k_k), pv(k_a), pv(r_k))


def _selection_matrices():
    n_rep = LANES // (2 * RW_HEADS)
    sel_k = np.zeros((2 * LANES, 8 * LANES), np.float32)
    sel_v = np.zeros((2 * LANES, 2 * LANES), np.float32)
    sel_y = np.zeros((2 * LANES, 2 * LANES), np.float32)
    for b in range(2):
        for i in range(8):
            for h in range(RW_HEADS):
                src = b * LANES + i * RW_HEADS + h
                for vl in range(n_rep):
                    sel_k[src, i * LANES + vl * 2 * RW_HEADS + b * RW_HEADS + h] = 1.0
                dst = (i // n_rep) * LANES + (i % n_rep) * 2 * RW_HEADS + b * RW_HEADS + h
                sel_v[src, dst] = 1.0
                sel_y[dst, src] = 1.0
    return jnp.asarray(sel_k, BF16), jnp.asarray(sel_v, BF16), jnp.asarray(sel_y, BF16)


def _split_bf16(x):
    hi = x.astype(BF16)
    return hi, (x - hi.astype(F32)).astype(BF16)


def _select(x, sel):
    hi, lo = _split_bf16(x)
    return (jnp.dot(hi, sel, preferred_element_type=F32) + jnp.dot(lo, sel, preferred_element_type=F32))


def _scan_kernel(w0, w1, b0, b1, k0, k1, r0, r1, a0, a1, v0, v1, selk_ref, selv_ref, sely_ref,
                 y_ref, rep, v3, y3, st):
    tc = v0.shape[2]
    n_oct = tc // 8
    OP_W, OP_B, OP_K, OP_R, OP_A = range(5)

    @pl.when(pl.program_id(0) == 0)
    def _():
        st[...] = jnp.zeros(st.shape, F32)

    selk = selk_ref[...]
    for op, (x0, x1) in enumerate(((w0, w1), (b0, b1), (k0, k1), (r0, r1), (a0, a1))):
        for m in range(SCAN_SLABS):
            piece = jnp.concatenate([x0[0, m], x1[0, m]], axis=1)
            if op == OP_W:
                laid = _select(piece, selk)
            else:
                laid = jnp.dot(piece.astype(BF16), selk, preferred_element_type=F32)
            rep[op, m] = laid.reshape(n_oct, 8, 8 * LANES)
    selv = selv_ref[...]
    for m in range(SCAN_SLABS):
        out = _select(jnp.concatenate([v0[0, m], v1[0, m]], axis=1), selv)
        v3[:, 2 * m, :] = out[:, 0:LANES]
        v3[:, 2 * m + 1, :] = out[:, LANES:2 * LANES]

    def row(op, k, o, s):
        m, kl = divmod(k, 8)
        return rep[op, m, o, s:s + 1, kl * LANES:(kl + 1) * LANES]

    def dot_state(op, o, s):
        acc = [jnp.zeros((SCAN_VHI, LANES), F32) for _ in range(2)]
        for k in range(RW_HEAD):
            acc[k % 2] = acc[k % 2] + st[k] * row(op, k, o, s)
        return acc[0] + acc[1]

    def octet(o, sa):
        o_next = jnp.minimum(o + 1, n_oct - 1)
        for s in range(8):
            t = o * 8 + s
            nxt = (o, s + 1) if s < 7 else (o_next, 0)
            vt = v3[t]
            accy = [jnp.zeros((SCAN_VHI, LANES), F32) for _ in range(2)]
            accs = [jnp.zeros((SCAN_VHI, LANES), F32) for _ in range(2)]
            for k in range(RW_HEAD):
                snew = st[k] * row(OP_W, k, o, s) + sa * row(OP_B, k, o, s) + vt * row(OP_K, k, o, s)
                st[k] = snew
                accy[k % 2] = accy[k % 2] + snew * row(OP_R, k, o, s)
                accs[k % 2] = accs[k % 2] + snew * row(OP_A, k, *nxt)
            y3[t] = accy[0] + accy[1]
            sa = accs[0] + accs[1]
        return sa

    lax.fori_loop(0, n_oct, octet, dot_state(OP_A, 0, 0))

    sely = sely_ref[...]
    for m in range(SCAN_SLABS):
        piece = jnp.concatenate([y3[:, 2 * m, :], y3[:, 2 * m + 1, :]], axis=1)
        both = _select(piece, sely)
        for b in range(2):
            y_ref[b, :, m * LANES:(m + 1) * LANES] = both[:, b * LANES:(b + 1) * LANES]


def _scan(r, w, k, v, a, b):
    bsz, _, seq, _ = r.shape
    assert bsz == 2
    tc = SCAN_CHUNK
    slab0 = pl.BlockSpec((1, SCAN_SLABS, tc, LANES), lambda i: (0, 0, i, 0))
    slab1 = pl.BlockSpec((1, SCAN_SLABS, tc, LANES), lambda i: (1, 0, i, 0))
    const = lambda *shape: pl.BlockSpec(shape, lambda i: (0,) * len(shape))
    sel_k, sel_v, sel_y = _selection_matrices()
    return pl.pallas_call(
        _scan_kernel,
        grid=(seq // tc,),
        in_specs=[slab0, slab1] * 6 + [const(*sel_k.shape), const(*sel_v.shape), const(*sel_y.shape)],
        out_specs=pl.BlockSpec((bsz, tc, D_MODEL), lambda i: (0, i, 0)),
        out_shape=jax.ShapeDtypeStruct((bsz, seq, D_MODEL), F32),
        scratch_shapes=[
            pltpu.VMEM((5, SCAN_SLABS, tc // 8, 8, 8 * LANES), F32),
            pltpu.VMEM((tc, SCAN_VHI, LANES), F32),
            pltpu.VMEM((tc, SCAN_VHI, LANES), F32),
            pltpu.VMEM((RW_HEAD, SCAN_VHI, LANES), F32),
        ],
        compiler_params=_cparams(("arbitrary",)),
        name="scan",
    )(w, w, b, b, k, k, r, r, a, a, v, v, sel_k, sel_v, sel_y)


def _odd_out_kernel(y_ref, bonus_ref, g_ref, x_ref, lnw_ref, lnb_ref, wo_ref, fn_ref, o_ref):
    y = y_ref[0]
    inv_n = 1.0 / RW_HEAD
    dlt = y - _tile8(_head_sum(y)) * inv_n
    var = _tile8(_head_sum(dlt * dlt)) * inv_n
    yn = dlt * lax.rsqrt(var + LNX_EPS) * lnw_ref[...] + lnb_ref[...]
    z = ((yn + bonus_ref[0]) * g_ref[0]).astype(BF16)
    x2 = x_ref[0] + jnp.dot(z, wo_ref[...], preferred_element_type=F32)
    o_ref[0] = _rms(x2, fn_ref[...])


def _odd_out(y, bonus, gs, x1, lnw, lnb, wo, fn):
    bsz, seq, _ = x1.shape
    tm = RESID_TILE
    perm = _index_major()
    pv = lambda p: p.reshape(-1)[perm].reshape(1, D_MODEL)
    row = pl.BlockSpec((1, tm, D_MODEL), lambda b, i: (b, i, 0))
    vec = pl.BlockSpec((1, D_MODEL), lambda b, i: (0, 0))
    return pl.pallas_call(
        _odd_out_kernel,
        grid=(bsz, seq // tm),
        in_specs=[row, row, row, row, vec, vec, pl.BlockSpec((D_MODEL, D_MODEL), lambda b, i: (0, 0)), vec],
        out_specs=row,
        out_shape=jax.ShapeDtypeStruct(x1.shape, F32),
        compiler_params=_cparams(("parallel", "parallel")),
        name="odd_out",
    )(y, bonus, gs, x1, pv(lnw), pv(lnb), wo[perm, :].astype(BF16), fn.reshape(1, D_MODEL))


def kernel(x, ln_even, w_in_even, gm_norm, gm_ws, gm_b, w_out_even, ln_odd, rw_mu, rw_wr, rw_wk,
           rw_wv, rw_wg, rw_w0, rw_w1, rw_w2, rw_a0, rw_a1, rw_a2, rw_kk, rw_ka, rw_rk, rw_lnw,
           rw_lnb, rw_wo, final_norm):
    assert x.shape[1] % ATT_TILE == 0 and x.shape[2] == D_MODEL
    assert ln_even.shape[0] == 1 and ln_odd.shape[0] == 1
    ya, zb, q, k, v = _even_in(x, ln_even[0], w_in_even[0], gm_norm[0], gm_ws[0], gm_b[0])
    b_out = _attention(q, k, v)
    x1 = _even_out(x, ya, b_out, zb, w_out_even[0])
    r, w, kmod, vv, a, b, gs, bonus = _odd_in(
        x1, ln_odd[0], rw_mu[0], rw_wr[0], rw_wk[0], rw_wv[0], rw_wg[0], rw_w0[0], rw_w1[0], rw_w2[0],
        rw_a0[0], rw_a1[0], rw_a2[0], rw_kk[0], rw_ka[0], rw_rk[0])
    y = _scan(r, w, kmod, vv, a, b)
    return _odd_out(y, bonus, gs, x1, rw_lnw[0], rw_lnb[0], rw_wo[0], final_norm)
```

```python
import functools
import math

import numpy as np
import jax
import jax.numpy as jnp
from jax import lax
from jax.experimental import pallas as pl
from jax.experimental.pallas import tpu as pltpu

F32 = jnp.float32
BF16 = jnp.bfloat16

D_MODEL = 1024
GM_GROUPS = 4
GM_CH = 128
GM_CHUNK = 128
GM_WIDTH = GM_GROUPS * GM_CH
ATT_HEADS = 8
ATT_HD = 64
ATT_WIDTH = ATT_HEADS * ATT_HD
ATT_BLOCK = 128
DILATIONS = (1, 4, 16)
ATT_TILE = ATT_BLOCK * 16
ATT_SLABS = ATT_WIDTH // 128
ATT_UNROLL = 16
IN_EVEN = 2 * GM_WIDTH + 3 * ATT_WIDTH + D_MODEL
RW_HEAD = 64
RW_HEADS = D_MODEL // RW_HEAD
RW_LORA = 64
RMS_EPS = 1e-6
LNX_EPS = 64e-5

LANES = 128
TOKEN_TILE = 256
RESID_TILE = 512
SCAN_CHUNK = 128
SCAN_VHI = RW_HEAD // 4
SCAN_SLABS = D_MODEL // LANES
VMEM_LIMIT = 56 * 1024 * 1024


def _cparams(sem):
    return pltpu.CompilerParams(dimension_semantics=sem, vmem_limit_bytes=VMEM_LIMIT)


def _gelu_tanh(x):
    return 0.5 * x * (1.0 + jnp.tanh(0.7978845608028654 * (x + 0.044715 * (x * x * x))))


def _silu(x):
    return x * jax.nn.sigmoid(x)


def _rms(x, g):
    return x * lax.rsqrt(jnp.mean(x * x, axis=-1, keepdims=True) + RMS_EPS) * g


def _even_in_kernel(x_ref, ln_ref, win_ref, gmn_ref, ws_ref, gmb_ref,
                    ya_ref, zb_ref, q_ref, k_ref, v_ref):
    tm = x_ref.shape[1]
    h = _rms(x_ref[0], ln_ref[...])
    proj = jnp.dot(h.astype(BF16), win_ref[...], preferred_element_type=F32)
    row = lax.broadcasted_iota(jnp.int32, (GM_CHUNK, GM_CHUNK), 0)
    col = lax.broadcasted_iota(jnp.int32, (GM_CHUNK, GM_CHUNK), 1)
    causal = row >= col
    z0 = 2 * GM_WIDTH + 3 * ATT_WIDTH
    sz = _silu(proj[:, z0:z0 + D_MODEL])
    for g in range(GM_GROUPS):
        u = _gelu_tanh(proj[:, g * GM_CH:(g + 1) * GM_CH])
        vg = _gelu_tanh(proj[:, GM_WIDTH + g * GM_CH:GM_WIDTH + (g + 1) * GM_CH])
        vn = vg * lax.rsqrt(jnp.mean(vg * vg, axis=-1, keepdims=True) + RMS_EPS) * gmn_ref[g:g + 1, :]
        wsg = jnp.where(causal, ws_ref[g], 0.0).astype(BF16)
        for c in range(tm // GM_CHUNK):
            rows = slice(c * GM_CHUNK, (c + 1) * GM_CHUNK)
            sp = jnp.dot(wsg, vn[rows].astype(BF16), preferred_element_type=F32) + gmb_ref[g]
            a = u[rows] * sp
            ya_ref[0, rows, g * GM_CH:(g + 1) * GM_CH] = (a * sz[rows, g * GM_CH:(g + 1) * GM_CH]).astype(BF16)
    q0 = 2 * GM_WIDTH
    for j in range(ATT_SLABS):
        lanes = slice(j * LANES, (j + 1) * LANES)
        zb_ref[0, j] = sz[:, GM_WIDTH + j * LANES:GM_WIDTH + (j + 1) * LANES]
        q_ref[0, j] = proj[:, q0:q0 + ATT_WIDTH][:, lanes] * (1.0 / math.sqrt(ATT_HD))
        k_ref[0, j] = proj[:, q0 + ATT_WIDTH:q0 + 2 * ATT_WIDTH][:, lanes]
        v_ref[0, j] = proj[:, q0 + 2 * ATT_WIDTH:q0 + 3 * ATT_WIDTH][:, lanes]


def _even_in(x, ln, w_in, gm_norm, gm_ws, gm_b):
    bsz, seq, _ = x.shape
    tm = TOKEN_TILE
    slab = jax.ShapeDtypeStruct((bsz, ATT_SLABS, seq, LANES), F32)
    slab_spec = pl.BlockSpec((1, ATT_SLABS, tm, LANES), lambda b, i: (b, 0, i, 0))
    const = lambda *shape: pl.BlockSpec(shape, lambda b, i: (0,) * len(shape))
    gmb = jnp.broadcast_to(gm_b[:, :, None], (GM_GROUPS, GM_CHUNK, GM_CH))
    return pl.pallas_call(
        _even_in_kernel,
        grid=(bsz, seq // tm),
        in_specs=[
            pl.BlockSpec((1, tm, D_MODEL), lambda b, i: (b, i, 0)),
            const(1, D_MODEL),
            const(D_MODEL, IN_EVEN),
            const(GM_GROUPS, GM_CH),
            const(GM_GROUPS, GM_CHUNK, GM_CHUNK),
            const(GM_GROUPS, GM_CHUNK, GM_CH),
        ],
        out_specs=[
            pl.BlockSpec((1, tm, GM_WIDTH), lambda b, i: (b, i, 0)),
            slab_spec, slab_spec, slab_spec, slab_spec,
        ],
        out_shape=[jax.ShapeDtypeStruct((bsz, seq, GM_WIDTH), BF16), slab, slab, slab, slab],
        compiler_params=_cparams(("parallel", "parallel")),
        name="even_in",
    )(x, ln.reshape(1, D_MODEL), w_in.astype(BF16), gm_norm, gm_ws, gmb)


def _attn_table():
    qi = np.arange(ATT_BLOCK)[:, None]
    kloc = np.arange(2 * ATT_BLOCK)[None, :] - ATT_BLOCK
    j = qi - kloc
    valid = (j >= 0) & (j <= ATT_BLOCK)
    slopes = 2.0 ** (-8.0 * np.arange(1, ATT_HEADS + 1) / ATT_HEADS)
    tabs = []
    for d in DILATIONS:
        bias = -slopes[:, None, None].astype(np.float32) * (j * d).astype(np.float32)[None]
        later = np.where(valid[None], bias, -np.inf)
        first = np.where(kloc[None] >= 0, later, -np.inf)
        tabs.append(np.stack([later, first], axis=1))
    return jnp.asarray(np.stack(tabs), dtype=F32)


def _attn_kernel(q_ref, kp_ref, kc_ref, vp_ref, vc_ref, tab_ref, o_ref,
                 kbuf, vbuf, sbuf, pbuf, acc, mm, mm_sw, ll_sw):
    tile = pl.program_id(2)
    kbuf[0:ATT_TILE, :] = kp_ref[0, 0]
    kbuf[ATT_TILE:2 * ATT_TILE, :] = kc_ref[0, 0]
    vbuf[0:ATT_TILE, :] = vp_ref[0, 0]
    vbuf[ATT_TILE:2 * ATT_TILE, :] = vc_ref[0, 0]
    head0_q = lax.broadcasted_iota(jnp.int32, (ATT_BLOCK, LANES), 1) < ATT_HD
    head0_kv = lax.broadcasted_iota(jnp.int32, (2 * ATT_BLOCK, LANES), 1) < ATT_HD
    n_blk = ATT_TILE // ATT_BLOCK

    for p, d in enumerate(DILATIONS):
        shift = int(math.log2(d))

        def rows(i, d=d, shift=shift):
            nl = lax.shift_right_logical(i, shift)
            r = jnp.bitwise_and(i, d - 1)
            qs = nl * (ATT_BLOCK * d) + r
            ks = qs + (ATT_TILE - ATT_BLOCK * d)
            if d == 1:
                return nl, pl.ds(qs, ATT_BLOCK), pl.ds(ks, 2 * ATT_BLOCK)
            return nl, pl.ds(qs, ATT_BLOCK, stride=d), pl.ds(ks, 2 * ATT_BLOCK, stride=d)

        def scores(i, carry, p=p, rows=rows):
            nl, qsl, ksl = rows(i)
            qf = q_ref[0, 0, qsl, :]
            kf = kbuf[ksl, :].astype(BF16)
            first = jnp.logical_and(tile == 0, nl == 0).astype(jnp.int32)
            qm = jnp.concatenate([jnp.where(head0_q, qf, 0.0), jnp.where(head0_q, 0.0, qf)], axis=0)
            s = lax.dot_general(qm.astype(BF16), kf, (((1,), (1,)), ((), ())), preferred_element_type=F32)
            for hh in range(2):
                sbuf[i, hh] = s[hh * ATT_BLOCK:(hh + 1) * ATT_BLOCK] + tab_ref[p, hh, first]
            return carry

        def softmax(i, carry, p=p, rows=rows):
            _, qsl, _ = rows(i)
            m = []
            for hh in range(2):
                s = sbuf[i, hh]
                m.append(jnp.max(s, axis=-1, keepdims=True))
                pbuf[i, hh] = jnp.exp(s - m[hh]).astype(BF16)
            mm[p, qsl, :] = jnp.where(head0_q, m[0], m[1])
            mm_sw[p, qsl, :] = jnp.where(head0_q, m[1], m[0])
            return carry

        def values(i, carry, p=p, rows=rows):
            _, qsl, ksl = rows(i)
            vf = vbuf[ksl, :]
            o0 = jnp.dot(pbuf[i, 0], jnp.where(head0_kv, vf, 1.0).astype(BF16), preferred_element_type=F32)
            o1 = jnp.dot(pbuf[i, 1], jnp.where(head0_kv, 1.0, vf).astype(BF16), preferred_element_type=F32)
            acc[p, qsl, :] = jnp.where(head0_q, o0, o1)
            ll_sw[p, qsl, :] = jnp.where(head0_q, o1, o0)
            return carry

        for stage in (scores, softmax, values):
            lax.fori_loop(0, n_blk, stage, 0, unroll=ATT_UNROLL)

    m_all = jnp.maximum(jnp.maximum(mm[0], mm[1]), mm[2])
    m_all_sw = jnp.maximum(jnp.maximum(mm_sw[0], mm_sw[1]), mm_sw[2])
    num = jnp.zeros((ATT_TILE, LANES), F32)
    den_sw = jnp.zeros((ATT_TILE, LANES), F32)
    for p in range(len(DILATIONS)):
        num = num + jnp.exp(mm[p] - m_all) * acc[p]
        den_sw = den_sw + jnp.exp(mm_sw[p] - m_all_sw) * ll_sw[p]
    o_ref[0, 0] = num / pltpu.roll(den_sw, ATT_HD, 1)


def _attention(q, k, v):
    bsz, _, seq, _ = q.shape
    blk = (1, 1, ATT_TILE, LANES)
    cur = pl.BlockSpec(blk, lambda b, j, t: (b, j, t, 0))
    prev = pl.BlockSpec(blk, lambda b, j, t: (b, j, jnp.maximum(t - 1, 0), 0))
    n_pat = len(DILATIONS)
    return pl.pallas_call(
        _attn_kernel,
        grid=(bsz, ATT_SLABS, seq // ATT_TILE),
        in_specs=[cur, prev, cur, prev, cur,
                  pl.BlockSpec((n_pat, 2, 2, ATT_BLOCK, 2 * ATT_BLOCK), lambda b, j, t: (0, j, 0, 0, 0))],
        out_specs=cur,
        out_shape=jax.ShapeDtypeStruct(q.shape, F32),
        scratch_shapes=[
            pltpu.VMEM((2 * ATT_TILE, LANES), F32),
            pltpu.VMEM((2 * ATT_TILE, LANES), F32),
            pltpu.VMEM((ATT_TILE // ATT_BLOCK, 2, ATT_BLOCK, 2 * ATT_BLOCK), F32),
            pltpu.VMEM((ATT_TILE // ATT_BLOCK, 2, ATT_BLOCK, 2 * ATT_BLOCK), BF16),
            pltpu.VMEM((n_pat, ATT_TILE, LANES), F32),
            pltpu.VMEM((n_pat, ATT_TILE, LANES), F32),
            pltpu.VMEM((n_pat, ATT_TILE, LANES), F32),
            pltpu.VMEM((n_pat, ATT_TILE, LANES), F32),
        ],
        compiler_params=_cparams(("parallel", "parallel", "parallel")),
        name="attention",
    )(q, k, k, v, v, _attn_table())


def _even_out_kernel(x_ref, ya_ref, bo_ref, zb_ref, wout_ref, o_ref):
    parts = [ya_ref[0]] + [(bo_ref[0, j] * zb_ref[0, j]).astype(BF16) for j in range(ATT_SLABS)]
    y = jnp.concatenate(parts, axis=1)
    o_ref[0] = x_ref[0] + jnp.dot(y, wout_ref[...], preferred_element_type=F32)


def _even_out(x, ya, b_out, zb, w_out):
    bsz, seq, _ = x.shape
    tm = RESID_TILE
    row = pl.BlockSpec((1, tm, D_MODEL), lambda b, i: (b, i, 0))
    slab_spec = pl.BlockSpec((1, ATT_SLABS, tm, LANES), lambda b, i: (b, 0, i, 0))
    return pl.pallas_call(
        _even_out_kernel,
        grid=(bsz, seq // tm),
        in_specs=[row, pl.BlockSpec((1, tm, GM_WIDTH), lambda b, i: (b, i, 0)), slab_spec, slab_spec,
                  pl.BlockSpec((D_MODEL, D_MODEL), lambda b, i: (0, 0))],
        out_specs=row,
        out_shape=jax.ShapeDtypeStruct(x.shape, F32),
        compiler_params=_cparams(("parallel", "parallel")),
        name="even_out",
    )(x, ya, b_out, zb, w_out.astype(BF16))


def _index_major(n=RW_HEAD, heads=RW_HEADS):
    new = np.arange(n * heads)
    return (new % heads) * n + new // heads


def _head_sum(t):
    s = t[:, 0:LANES]
    for c in range(1, D_MODEL // LANES):
        s = s + t[:, c * LANES:(c + 1) * LANES]
    for sh in (RW_HEADS, 2 * RW_HEADS, 4 * RW_HEADS):
        s = s + pltpu.roll(s, sh, 1)
    return s


def _tile8(s):
    return jnp.concatenate([s] * (D_MODEL // LANES), axis=1)


def _odd_in_kernel(x_ref, halo_ref, ln_ref, mu_ref, wr_ref, wk_ref, wv_ref, wg_ref,
                   w1_ref, w2_ref, a1_ref, a2_ref, w0_ref, a0_ref, kk_ref, ka_ref, rk_ref,
                   r_out, w_out, k_out, v_out, a_out, b_out, g_out, bonus_out):
    tm = x_ref.shape[1]
    ln = ln_ref[...]
    h = _rms(x_ref[0], ln)
    h_halo = _rms(halo_ref[0], ln)
    last = jnp.where(pl.program_id(1) == 0, 0.0, h_halo[7:8, :])
    first_row = lax.broadcasted_iota(jnp.int32, (tm, D_MODEL), 0) == 0
    hprev = jnp.where(first_row, last, pltpu.roll(h, 1, 0))
    xx = hprev - h
    mix = lambda i: (h + xx * mu_ref[i:i + 1, :]).astype(BF16)
    dot = lambda a, b: jnp.dot(a, b, preferred_element_type=F32)
    r = dot(mix(0), wr_ref[...])
    lw = dot(jnp.tanh(dot(mix(1), w1_ref[...])).astype(BF16), w2_ref[...])
    k = dot(mix(2), wk_ref[...])
    v = dot(mix(3), wv_ref[...])
    la = dot(dot(mix(4), a1_ref[...]).astype(BF16), a2_ref[...])
    g = dot(mix(5), wg_ref[...])
    z = -(w0_ref[...] + lw)
    softplus = jnp.maximum(z, 0.0) + jnp.log1p(jnp.exp(-jnp.abs(z)))
    decay = jnp.exp(-jnp.exp(-softplus - 0.5))
    a = jax.nn.sigmoid(a0_ref[...] + la)
    kk = k * kk_ref[...]
    kk = kk * _tile8(lax.rsqrt(jnp.maximum(_head_sum(kk * kk), 1e-24)))
    kmod = k * (1.0 + (a - 1.0) * ka_ref[...])
    bonus = _tile8(_head_sum(r * kmod * rk_ref[...])) * v
    step = lax.broadcasted_iota(jnp.int32, (tm, LANES), 0)
    in_octet = jnp.bitwise_and(step, 7)
    keep = [in_octet >= sh for sh in (1, 2, 4)]
    chunk_start = jnp.bitwise_and(step, SCAN_CHUNK - 1) == 0
    kkb = kk * a
    for m in range(SCAN_SLABS):
        lanes = slice(m * LANES, (m + 1) * LANES)
        cum = decay[:, lanes]
        for sh, mask in zip((1, 2, 4), keep):
            cum = cum * jnp.where(mask, pltpu.roll(cum, sh, 0), 1.0)
        cum_before = jnp.where(chunk_start, 1.0, pltpu.roll(cum, 1, 0))
        inv_cum = 1.0 / cum
        r_out[0, m] = r[:, lanes] * cum
        w_out[0, m] = cum
        k_out[0, m] = kmod[:, lanes] * inv_cum
        v_out[0, m] = v[:, lanes]
        a_out[0, m] = -(kk[:, lanes] * cum_before)
        b_out[0, m] = kkb[:, lanes] * inv_cum
    g_out[0] = _silu(g)
    bonus_out[0] = bonus


def _odd_in(x1, ln, mu, wr, wk, wv, wg, w0, w1, w2, a0, a1, a2, k_k, k_a, r_k):
    bsz, seq, _ = x1.shape
    tm = TOKEN_TILE
    assert tm % SCAN_CHUNK == 0
    perm = _index_major()
    pw = lambda w: w[:, perm].astype(BF16)
    pv = lambda p: p.reshape(-1)[perm].reshape(1, D_MODEL)
    row = pl.BlockSpec((1, tm, D_MODEL), lambda b, i: (b, i, 0))
    halo = pl.BlockSpec((1, 8, D_MODEL), lambda b, i: (b, jnp.maximum(i * (tm // 8) - 1, 0), 0))
    const = lambda *shape: pl.BlockSpec(shape, lambda b, i: (0,) * len(shape))
    mat, vec = const(D_MODEL, D_MODEL), const(1, D_MODEL)
    out = jax.ShapeDtypeStruct(x1.shape, F32)
    slab = jax.ShapeDtypeStruct((bsz, SCAN_SLABS, seq, LANES), F32)
    slab_spec = pl.BlockSpec((1, SCAN_SLABS, tm, LANES), lambda b, i: (b, 0, i, 0))
    return pl.pallas_call(
        _odd_in_kernel,
        grid=(bsz, seq // tm),
        in_specs=[row, halo, vec, const(6, D_MODEL), mat, mat, mat, mat,
                  const(D_MODEL, RW_LORA), const(RW_LORA, D_MODEL),
                  const(D_MODEL, RW_LORA), const(RW_LORA, D_MODEL),
                  vec, vec, vec, vec, vec],
        out_specs=[slab_spec] * 6 + [row] * 2,
        out_shape=[slab] * 6 + [out] * 2,
        compiler_params=_cparams(("parallel", "arbitrary")),
        name="odd_in",
    )(x1, x1, ln.reshape(1, D_MODEL), mu, pw(wr), pw(wk), pw(wv), pw(wg),
      w1.astype(BF16), pw(w2), a1.astype(BF16), pw(a2),
      pv(w0), pv(a0), pv(k_k), pv(k_a), pv(r_k))


def _selection_matrices():
    n_rep = LANES // (2 * RW_HEADS)
    sel_k = np.zeros((2 * LANES, 8 * LANES), np.float32)
    sel_v = np.zeros((2 * LANES, 2 * LANES), np.float32)
    sel_y = np.zeros((2 * LANES, 2 * LANES), np.float32)
    for b in range(2):
        for i in range(8):
            for h in range(RW_HEADS):
                src = b * LANES + i * RW_HEADS + h
                for vl in range(n_rep):
                    sel_k[src, i * LANES + vl * 2 * RW_HEADS + b * RW_HEADS + h] = 1.0
                dst = (i // n_rep) * LANES + (i % n_rep) * 2 * RW_HEADS + b * RW_HEADS + h
                sel_v[src, dst] = 1.0
                sel_y[dst, src] = 1.0
    return jnp.asarray(sel_k, BF16), jnp.asarray(sel_v, BF16), jnp.asarray(sel_y, BF16)


def _split_bf16(x):
    hi = x.astype(BF16)
    return hi, (x - hi.astype(F32)).astype(BF16)


def _select(x, sel):
    hi, lo = _split_bf16(x)
    return (jnp.dot(hi, sel, preferred_element_type=F32) + jnp.dot(lo, sel, preferred_element_type=F32))


def _scan_kernel(g0, g1, b0, b1, k0, k1, r0, r1, a0, a1, v0, v1, selk_ref, selv_ref, sely_ref,
                 y_ref, rep, rep_g, g_rows0, g_rows1, v3, y3, st):
    tc = v0.shape[2]
    n_oct = tc // 8
    OP_B, OP_K, OP_R, OP_A = range(4)

    @pl.when(pl.program_id(0) == 0)
    def _():
        st[...] = jnp.zeros(st.shape, F32)

    selk = selk_ref[...]
    for op, (x0, x1) in enumerate(((b0, b1), (k0, k1), (r0, r1), (a0, a1))):
        for m in range(SCAN_SLABS):
            piece = jnp.concatenate([x0[0, m], x1[0, m]], axis=1)
            laid = jnp.dot(piece.astype(BF16), selk, preferred_element_type=F32)
            rep[op, m] = laid.reshape(n_oct, 8, 8 * LANES)
    for m in range(SCAN_SLABS):
        g_rows0[pl.ds(m, n_oct, stride=SCAN_SLABS), :] = g0[0, m, pl.ds(7, n_oct, stride=8), :]
        g_rows1[pl.ds(m, n_oct, stride=SCAN_SLABS), :] = g1[0, m, pl.ds(7, n_oct, stride=8), :]
    g_last = jnp.concatenate([g_rows0[...], g_rows1[...]], axis=1)
    rep_g[...] = _select(g_last, selk).reshape(n_oct, SCAN_SLABS, 8 * LANES)
    selv = selv_ref[...]
    for m in range(SCAN_SLABS):
        out = _select(jnp.concatenate([v0[0, m], v1[0, m]], axis=1), selv)
        v3[:, 2 * m, :] = out[:, 0:LANES]
        v3[:, 2 * m + 1, :] = out[:, LANES:2 * LANES]

    def row(op, k, o, s):
        m, kl = divmod(k, 8)
        return rep[op, m, o, s:s + 1, kl * LANES:(kl + 1) * LANES]

    def dot_state(op, o, s):
        acc = [jnp.zeros((SCAN_VHI, LANES), F32) for _ in range(2)]
        for k in range(RW_HEAD):
            acc[k % 2] = acc[k % 2] + st[k] * row(op, k, o, s)
        return acc[0] + acc[1]

    def octet(o, sa):
        o_next = jnp.minimum(o + 1, n_oct - 1)
        for s in range(8):
            t = o * 8 + s
            nxt = (o, s + 1) if s < 7 else (o_next, 0)
            vt = v3[t]
            accy = [jnp.zeros((SCAN_VHI, LANES), F32) for _ in range(2)]
            accs = [jnp.zeros((SCAN_VHI, LANES), F32) for _ in range(2)]
            for k in range(RW_HEAD):
                snew = st[k] + sa * row(OP_B, k, o, s) + vt * row(OP_K, k, o, s)
                st[k] = snew
                accy[k % 2] = accy[k % 2] + snew * row(OP_R, k, o, s)
                accs[k % 2] = accs[k % 2] + snew * row(OP_A, k, *nxt)
            y3[t] = accy[0] + accy[1]
            sa = accs[0] + accs[1]
        for k in range(RW_HEAD):
            m, kl = divmod(k, 8)
            st[k] = st[k] * rep_g[o, m:m + 1, kl * LANES:(kl + 1) * LANES]
        return sa

    lax.fori_loop(0, n_oct, octet, dot_state(OP_A, 0, 0))

    sely = sely_ref[...]
    for m in range(SCAN_SLABS):
        piece = jnp.concatenate([y3[:, 2 * m, :], y3[:, 2 * m + 1, :]], axis=1)
        both = _select(piece, sely)
        for b in range(2):
            y_ref[b, :, m * LANES:(m + 1) * LANES] = both[:, b * LANES:(b + 1) * LANES]


def _scan(r, w, k, v, a, b):
    bsz, _, seq, _ = r.shape
    assert bsz == 2
    tc = SCAN_CHUNK
    slab0 = pl.BlockSpec((1, SCAN_SLABS, tc, LANES), lambda i: (0, 0, i, 0))
    slab1 = pl.BlockSpec((1, SCAN_SLABS, tc, LANES), lambda i: (1, 0, i, 0))
    const = lambda *shape: pl.BlockSpec(shape, lambda i: (0,) * len(shape))
    sel_k, sel_v, sel_y = _selection_matrices()
    return pl.pallas_call(
        _scan_kernel,
        grid=(seq // tc,),
        in_specs=[slab0, slab1] * 6 + [const(*sel_k.shape), const(*sel_v.shape), const(*sel_y.shape)],
        out_specs=pl.BlockSpec((bsz, tc, D_MODEL), lambda i: (0, i, 0)),
        out_shape=jax.ShapeDtypeStruct((bsz, seq, D_MODEL), F32),
        scratch_shapes=[
            pltpu.VMEM((4, SCAN_SLABS, tc // 8, 8, 8 * LANES), F32),
            pltpu.VMEM((tc // 8, SCAN_SLABS, 8 * LANES), F32),
            pltpu.VMEM((tc // 8 * SCAN_SLABS, LANES), F32),
            pltpu.VMEM((tc // 8 * SCAN_SLABS, LANES), F32),
            pltpu.VMEM((tc, SCAN_VHI, LANES), F32),
            pltpu.VMEM((tc, SCAN_VHI, LANES), F32),
            pltpu.VMEM((RW_HEAD, SCAN_VHI, LANES), F32),
        ],
        compiler_params=_cparams(("arbitrary",)),
        name="scan",
    )(w, w, b, b, k, k, r, r, a, a, v, v, sel_k, sel_v, sel_y)


def _odd_out_kernel(y_ref, bonus_ref, g_ref, x_ref, lnw_ref, lnb_ref, wo_ref, fn_ref, o_ref):
    y = y_ref[0]
    inv_n = 1.0 / RW_HEAD
    dlt = y - _tile8(_head_sum(y)) * inv_n
    var = _tile8(_head_sum(dlt * dlt)) * inv_n
    yn = dlt * lax.rsqrt(var + LNX_EPS) * lnw_ref[...] + lnb_ref[...]
    z = ((yn + bonus_ref[0]) * g_ref[0]).astype(BF16)
    x2 = x_ref[0] + jnp.dot(z, wo_ref[...], preferred_element_type=F32)
    o_ref[0] = _rms(x2, fn_ref[...])


def _odd_out(y, bonus, gs, x1, lnw, lnb, wo, fn):
    bsz, seq, _ = x1.shape
    tm = RESID_TILE
    perm = _index_major()
    pv = lambda p: p.reshape(-1)[perm].reshape(1, D_MODEL)
    row = pl.BlockSpec((1, tm, D_MODEL), lambda b, i: (b, i, 0))
    vec = pl.BlockSpec((1, D_MODEL), lambda b, i: (0, 0))
    return pl.pallas_call(
        _odd_out_kernel,
        grid=(bsz, seq // tm),
        in_specs=[row, row, row, row, vec, vec, pl.BlockSpec((D_MODEL, D_MODEL), lambda b, i: (0, 0)), vec],
        out_specs=row,
        out_shape=jax.ShapeDtypeStruct(x1.shape, F32),
        compiler_params=_cparams(("parallel", "parallel")),
        name="odd_out",
    )(y, bonus, gs, x1, pv(lnw), pv(lnb), wo[perm, :].astype(BF16), fn.reshape(1, D_MODEL))


def kernel(x, ln_even, w_in_even, gm_norm, gm_ws, gm_b, w_out_even, ln_odd, rw_mu, rw_wr, rw_wk,
           rw_wv, rw_wg, rw_w0, rw_w1, rw_w2, rw_a0, rw_a1, rw_a2, rw_kk, rw_ka, rw_rk, rw_lnw,
           rw_lnb, rw_wo, final_norm):
    assert x.shape[1] % ATT_TILE == 0 and x.shape[2] == D_MODEL
    assert ln_even.shape[0] == 1 and ln_odd.shape[0] == 1
    ya, zb, q, k, v = _even_in(x, ln_even[0], w_in_even[0], gm_norm[0], gm_ws[0], gm_b[0])
    b_out = _attention(q, k, v)
    x1 = _even_out(x, ya, b_out, zb, w_out_even[0])
    r, w, kmod, vv, a, b, gs, bonus = _odd_in(
        x1, ln_odd[0], rw_mu[0], rw_wr[0], rw_wk[0], rw_wv[0], rw_wg[0], rw_w0[0], rw_w1[0], rw_w2[0],
        rw_a0[0], rw_a1[0], rw_a2[0], rw_kk[0], rw_ka[0], rw_rk[0])
    y = _scan(r, w, kmod, vv, a, b)
    return _odd_out(y, bonus, gs, x1, rw_lnw[0], rw_lnb[0], rw_wo[0], final_norm)
```

```python
import math

import numpy as np
import jax
import jax.numpy as jnp
from jax import lax
from jax.experimental import pallas as pl
from jax.experimental.pallas import tpu as pltpu

F32 = jnp.float32
BF16 = jnp.bfloat16

D_MODEL = 1024
GM_GROUPS = 4
GM_CH = 128
GM_CHUNK = 128
GM_WIDTH = GM_GROUPS * GM_CH
ATT_HEADS = 8
ATT_HD = 64
ATT_WIDTH = ATT_HEADS * ATT_HD
ATT_BLOCK = 128
DILATIONS = (1, 4, 16)
ATT_TILE = ATT_BLOCK * 16
ATT_SLABS = ATT_WIDTH // 128
ATT_UNROLL = 16
IN_EVEN = 2 * GM_WIDTH + 3 * ATT_WIDTH + D_MODEL
RW_HEAD = 64
RW_HEADS = D_MODEL // RW_HEAD
RW_LORA = 64
RMS_EPS = 1e-6
LNX_EPS = 64e-5

LANES = 128
TOKEN_TILE = 256
RESID_TILE = 512
SCAN_CHUNK = 128
SCAN_VHI = RW_HEAD // 4
SCAN_SLABS = D_MODEL // LANES
VMEM_LIMIT = 56 * 1024 * 1024


def _cparams(sem):
    return pltpu.CompilerParams(dimension_semantics=sem, vmem_limit_bytes=VMEM_LIMIT)


def _gelu_tanh(x):
    return 0.5 * x * (1.0 + jnp.tanh(0.7978845608028654 * (x + 0.044715 * (x * x * x))))


def _silu(x):
    return x * jax.nn.sigmoid(x)


def _rms(x, g):
    return x * lax.rsqrt(jnp.mean(x * x, axis=-1, keepdims=True) + RMS_EPS) * g


def _even_in_kernel(x_ref, ln_ref, win_ref, gmn_ref, ws_ref, gmb_ref,
                    ya_ref, zb_ref, q_ref, k_ref, v_ref):
    tm = x_ref.shape[1]
    h = _rms(x_ref[0], ln_ref[...])
    proj = jnp.dot(h.astype(BF16), win_ref[...], preferred_element_type=F32)
    row = lax.broadcasted_iota(jnp.int32, (GM_CHUNK, GM_CHUNK), 0)
    col = lax.broadcasted_iota(jnp.int32, (GM_CHUNK, GM_CHUNK), 1)
    causal = row >= col
    z0 = 2 * GM_WIDTH + 3 * ATT_WIDTH
    sz = _silu(proj[:, z0:z0 + D_MODEL])
    for g in range(GM_GROUPS):
        u = _gelu_tanh(proj[:, g * GM_CH:(g + 1) * GM_CH])
        vg = _gelu_tanh(proj[:, GM_WIDTH + g * GM_CH:GM_WIDTH + (g + 1) * GM_CH])
        vn = vg * lax.rsqrt(jnp.mean(vg * vg, axis=-1, keepdims=True) + RMS_EPS) * gmn_ref[g:g + 1, :]
        wsg = jnp.where(causal, ws_ref[g], 0.0).astype(BF16)
        for c in range(tm // GM_CHUNK):
            rows = slice(c * GM_CHUNK, (c + 1) * GM_CHUNK)
            sp = jnp.dot(wsg, vn[rows].astype(BF16), preferred_element_type=F32) + gmb_ref[g]
            a = u[rows] * sp
            ya_ref[0, rows, g * GM_CH:(g + 1) * GM_CH] = (a * sz[rows, g * GM_CH:(g + 1) * GM_CH]).astype(BF16)
    q0 = 2 * GM_WIDTH
    for j in range(ATT_SLABS):
        lanes = slice(j * LANES, (j + 1) * LANES)
        zb_ref[0, j] = sz[:, GM_WIDTH + j * LANES:GM_WIDTH + (j + 1) * LANES]
        q_ref[0, j] = proj[:, q0:q0 + ATT_WIDTH][:, lanes] * (1.0 / math.sqrt(ATT_HD))
        k_ref[0, j] = proj[:, q0 + ATT_WIDTH:q0 + 2 * ATT_WIDTH][:, lanes]
        v_ref[0, j] = proj[:, q0 + 2 * ATT_WIDTH:q0 + 3 * ATT_WIDTH][:, lanes]


def _even_in(x, ln, w_in, gm_norm, gm_ws, gm_b):
    bsz, seq, _ = x.shape
    tm = TOKEN_TILE
    slab = jax.ShapeDtypeStruct((bsz, ATT_SLABS, seq, LANES), F32)
    slab_spec = pl.BlockSpec((1, ATT_SLABS, tm, LANES), lambda b, i: (b, 0, i, 0))
    const = lambda *shape: pl.BlockSpec(shape, lambda b, i: (0,) * len(shape))
    gmb = jnp.broadcast_to(gm_b[:, :, None], (GM_GROUPS, GM_CHUNK, GM_CH))
    return pl.pallas_call(
        _even_in_kernel,
        grid=(bsz, seq // tm),
        in_specs=[
            pl.BlockSpec((1, tm, D_MODEL), lambda b, i: (b, i, 0)),
            const(1, D_MODEL),
            const(D_MODEL, IN_EVEN),
            const(GM_GROUPS, GM_CH),
            const(GM_GROUPS, GM_CHUNK, GM_CHUNK),
            const(GM_GROUPS, GM_CHUNK, GM_CH),
        ],
        out_specs=[
            pl.BlockSpec((1, tm, GM_WIDTH), lambda b, i: (b, i, 0)),
            slab_spec, slab_spec, slab_spec, slab_spec,
        ],
        out_shape=[jax.ShapeDtypeStruct((bsz, seq, GM_WIDTH), BF16), slab, slab, slab, slab],
        compiler_params=_cparams(("parallel", "parallel")),
        name="even_in",
    )(x, ln.reshape(1, D_MODEL), w_in.astype(BF16), gm_norm, gm_ws, gmb)


def _attn_table():
    qi = np.arange(ATT_BLOCK)[:, None]
    kloc = np.arange(2 * ATT_BLOCK)[None, :] - ATT_BLOCK
    j = qi - kloc
    valid = (j >= 0) & (j <= ATT_BLOCK)
    slopes = 2.0 ** (-8.0 * np.arange(1, ATT_HEADS + 1) / ATT_HEADS)
    tabs = []
    for d in DILATIONS:
        bias = -slopes[:, None, None].astype(np.float32) * (j * d).astype(np.float32)[None]
        later = np.where(valid[None], bias, -np.inf)
        first = np.where(kloc[None] >= 0, later, -np.inf)
        tabs.append(np.stack([later, first], axis=1))
    return jnp.asarray(np.stack(tabs), dtype=F32)


def _attn_kernel(q_ref, kp_ref, kc_ref, vp_ref, vc_ref, tab_ref, o_ref,
                 kbuf, vbuf, sbuf, pbuf, acc, mm, ll_sw):
    tile = pl.program_id(2)
    kbuf[0:ATT_TILE, :] = kp_ref[0, 0]
    kbuf[ATT_TILE:2 * ATT_TILE, :] = kc_ref[0, 0]
    vbuf[0:ATT_TILE, :] = vp_ref[0, 0]
    vbuf[ATT_TILE:2 * ATT_TILE, :] = vc_ref[0, 0]
    head0_q = lax.broadcasted_iota(jnp.int32, (ATT_BLOCK, LANES), 1) < ATT_HD
    head0_kv = lax.broadcasted_iota(jnp.int32, (2 * ATT_BLOCK, LANES), 1) < ATT_HD
    n_blk = ATT_TILE // ATT_BLOCK

    for p, d in enumerate(DILATIONS):
        shift = int(math.log2(d))

        def rows(i, d=d, shift=shift):
            nl = lax.shift_right_logical(i, shift)
            r = jnp.bitwise_and(i, d - 1)
            qs = nl * (ATT_BLOCK * d) + r
            ks = qs + (ATT_TILE - ATT_BLOCK * d)
            if d == 1:
                return nl, pl.ds(qs, ATT_BLOCK), pl.ds(ks, 2 * ATT_BLOCK)
            return nl, pl.ds(qs, ATT_BLOCK, stride=d), pl.ds(ks, 2 * ATT_BLOCK, stride=d)

        def scores(i, carry, p=p, rows=rows):
            nl, qsl, ksl = rows(i)
            qf = q_ref[0, 0, qsl, :]
            kf = kbuf[ksl, :].astype(BF16)
            first = jnp.logical_and(tile == 0, nl == 0).astype(jnp.int32)
            qm = jnp.concatenate([jnp.where(head0_q, qf, 0.0), jnp.where(head0_q, 0.0, qf)], axis=0)
            s = lax.dot_general(qm.astype(BF16), kf, (((1,), (1,)), ((), ())), preferred_element_type=F32)
            for hh in range(2):
                sbuf[i, hh] = s[hh * ATT_BLOCK:(hh + 1) * ATT_BLOCK] + tab_ref[p, hh, first]
            return carry

        def softmax(i, carry, p=p, rows=rows):
            _, qsl, _ = rows(i)
            m = []
            for hh in range(2):
                s = sbuf[i, hh]
                m.append(jnp.max(s, axis=-1, keepdims=True))
                pbuf[i, hh] = jnp.exp(s - m[hh]).astype(BF16)
            mm[p, qsl, :] = jnp.where(head0_q, m[0], m[1])
            return carry

        def values(i, carry, p=p, rows=rows):
            _, qsl, ksl = rows(i)
            vf = vbuf[ksl, :]
            o0 = jnp.dot(pbuf[i, 0], jnp.where(head0_kv, vf, 1.0).astype(BF16), preferred_element_type=F32)
            o1 = jnp.dot(pbuf[i, 1], jnp.where(head0_kv, 1.0, vf).astype(BF16), preferred_element_type=F32)
            acc[p, qsl, :] = jnp.where(head0_q, o0, o1)
            ll_sw[p, qsl, :] = jnp.where(head0_q, o1, o0)
            return carry

        for stage in (scores, softmax, values):
            lax.fori_loop(0, n_blk, stage, 0, unroll=ATT_UNROLL)

    m_all = jnp.maximum(jnp.maximum(mm[0], mm[1]), mm[2])
    num = jnp.zeros((ATT_TILE, LANES), F32)
    den = jnp.zeros((ATT_TILE, LANES), F32)
    for p in range(len(DILATIONS)):
        e = jnp.exp(mm[p] - m_all)
        num = num + e * acc[p]
        den = den + e * pltpu.roll(ll_sw[p], ATT_HD, 1)
    o_ref[0, 0] = num / den


def _attention(q, k, v):
    bsz, _, seq, _ = q.shape
    blk = (1, 1, ATT_TILE, LANES)
    cur = pl.BlockSpec(blk, lambda b, j, t: (b, j, t, 0))
    prev = pl.BlockSpec(blk, lambda b, j, t: (b, j, jnp.maximum(t - 1, 0), 0))
    n_pat = len(DILATIONS)
    return pl.pallas_call(
        _attn_kernel,
        grid=(bsz, ATT_SLABS, seq // ATT_TILE),
        in_specs=[cur, prev, cur, prev, cur,
                  pl.BlockSpec((n_pat, 2, 2, ATT_BLOCK, 2 * ATT_BLOCK), lambda b, j, t: (0, j, 0, 0, 0))],
        out_specs=cur,
        out_shape=jax.ShapeDtypeStruct(q.shape, F32),
        scratch_shapes=[
            pltpu.VMEM((2 * ATT_TILE, LANES), F32),
            pltpu.VMEM((2 * ATT_TILE, LANES), F32),
            pltpu.VMEM((ATT_TILE // ATT_BLOCK, 2, ATT_BLOCK, 2 * ATT_BLOCK), F32),
            pltpu.VMEM((ATT_TILE // ATT_BLOCK, 2, ATT_BLOCK, 2 * ATT_BLOCK), BF16),
            pltpu.VMEM((n_pat, ATT_TILE, LANES), F32),
            pltpu.VMEM((n_pat, ATT_TILE, LANES), F32),
            pltpu.VMEM((n_pat, ATT_TILE, LANES), F32),
        ],
        compiler_params=_cparams(("parallel", "parallel", "parallel")),
        name="attention",
    )(q, k, k, v, v, _attn_table())


def _even_out_kernel(x_ref, ya_ref, bo_ref, zb_ref, wout_ref, o_ref):
    parts = [ya_ref[0]] + [(bo_ref[0, j] * zb_ref[0, j]).astype(BF16) for j in range(ATT_SLABS)]
    y = jnp.concatenate(parts, axis=1)
    o_ref[0] = x_ref[0] + jnp.dot(y, wout_ref[...], preferred_element_type=F32)


def _even_out(x, ya, b_out, zb, w_out):
    bsz, seq, _ = x.shape
    tm = RESID_TILE
    row = pl.BlockSpec((1, tm, D_MODEL), lambda b, i: (b, i, 0))
    slab_spec = pl.BlockSpec((1, ATT_SLABS, tm, LANES), lambda b, i: (b, 0, i, 0))
    return pl.pallas_call(
        _even_out_kernel,
        grid=(bsz, seq // tm),
        in_specs=[row, pl.BlockSpec((1, tm, GM_WIDTH), lambda b, i: (b, i, 0)), slab_spec, slab_spec,
                  pl.BlockSpec((D_MODEL, D_MODEL), lambda b, i: (0, 0))],
        out_specs=row,
        out_shape=jax.ShapeDtypeStruct(x.shape, F32),
        compiler_params=_cparams(("parallel", "parallel")),
        name="even_out",
    )(x, ya, b_out, zb, w_out.astype(BF16))


def _index_major(n=RW_HEAD, heads=RW_HEADS):
    new = np.arange(n * heads)
    return (new % heads) * n + new // heads


def _head_sum(t):
    s = t[:, 0:LANES]
    for c in range(1, D_MODEL // LANES):
        s = s + t[:, c * LANES:(c + 1) * LANES]
    for sh in (RW_HEADS, 2 * RW_HEADS, 4 * RW_HEADS):
        s = s + pltpu.roll(s, sh, 1)
    return s


def _tile8(s):
    return jnp.concatenate([s] * (D_MODEL // LANES), axis=1)


def _odd_in_kernel(x_ref, halo_ref, ln_ref, mu_ref, wr_ref, wk_ref, wv_ref, wg_ref,
                   w1_ref, w2_ref, a1_ref, a2_ref, w0_ref, a0_ref, kk_ref, ka_ref, rk_ref,
                   r_out, w_out, k_out, v_out, a_out, b_out, g_out, bonus_out):
    tm = x_ref.shape[1]
    ln = ln_ref[...]
    h = _rms(x_ref[0], ln)
    h_halo = _rms(halo_ref[0], ln)
    last = jnp.where(pl.program_id(1) == 0, 0.0, h_halo[7:8, :])
    first_row = lax.broadcasted_iota(jnp.int32, (tm, D_MODEL), 0) == 0
    hprev = jnp.where(first_row, last, pltpu.roll(h, 1, 0))
    xx = hprev - h
    mix = lambda i: (h + xx * mu_ref[i:i + 1, :]).astype(BF16)
    dot = lambda a, b: jnp.dot(a, b, preferred_element_type=F32)
    r = dot(mix(0), wr_ref[...])
    lw = dot(jnp.tanh(dot(mix(1), w1_ref[...])).astype(BF16), w2_ref[...])
    k = dot(mix(2), wk_ref[...])
    v = dot(mix(3), wv_ref[...])
    la = dot(dot(mix(4), a1_ref[...]).astype(BF16), a2_ref[...])
    g = dot(mix(5), wg_ref[...])
    decay = jnp.exp(-math.exp(-0.5) * jax.nn.sigmoid(w0_ref[...] + lw))
    a = jax.nn.sigmoid(a0_ref[...] + la)
    kk = k * kk_ref[...]
    kk = kk * _tile8(lax.rsqrt(jnp.maximum(_head_sum(kk * kk), 1e-24)))
    kmod = k * (1.0 + (a - 1.0) * ka_ref[...])
    bonus = _tile8(_head_sum(r * kmod * rk_ref[...])) * v
    step = lax.broadcasted_iota(jnp.int32, (tm, LANES), 0)
    in_octet = jnp.bitwise_and(step, 7)
    keep = [in_octet >= sh for sh in (1, 2, 4)]
    chunk_start = jnp.bitwise_and(step, SCAN_CHUNK - 1) == 0
    kkb = kk * a
    for m in range(SCAN_SLABS):
        lanes = slice(m * LANES, (m + 1) * LANES)
        cum = decay[:, lanes]
        for sh, mask in zip((1, 2, 4), keep):
            cum = cum * jnp.where(mask, pltpu.roll(cum, sh, 0), 1.0)
        cum_before = jnp.where(chunk_start, 1.0, pltpu.roll(cum, 1, 0))
        inv_cum = 1.0 / cum
        r_out[0, m] = r[:, lanes] * cum
        w_out[0, m] = cum
        k_out[0, m] = kmod[:, lanes] * inv_cum
        v_out[0, m] = v[:, lanes]
        a_out[0, m] = -(kk[:, lanes] * cum_before)
        b_out[0, m] = kkb[:, lanes] * inv_cum
    g_out[0] = _silu(g).astype(BF16)
    bonus_out[0] = bonus.astype(BF16)


def _odd_in(x1, ln, mu, wr, wk, wv, wg, w0, w1, w2, a0, a1, a2, k_k, k_a, r_k):
    bsz, seq, _ = x1.shape
    tm = TOKEN_TILE
    assert tm % SCAN_CHUNK == 0
    perm = _index_major()
    pw = lambda w: w[:, perm].astype(BF16)
    pv = lambda p: p.reshape(-1)[perm].reshape(1, D_MODEL)
    row = pl.BlockSpec((1, tm, D_MODEL), lambda b, i: (b, i, 0))
    halo = pl.BlockSpec((1, 8, D_MODEL), lambda b, i: (b, jnp.maximum(i * (tm // 8) - 1, 0), 0))
    const = lambda *shape: pl.BlockSpec(shape, lambda b, i: (0,) * len(shape))
    mat, vec = const(D_MODEL, D_MODEL), const(1, D_MODEL)
    slab = jax.ShapeDtypeStruct((bsz, SCAN_SLABS, seq, LANES), F32)
    slab_spec = pl.BlockSpec((1, SCAN_SLABS, tm, LANES), lambda b, i: (b, 0, i, 0))
    return pl.pallas_call(
        _odd_in_kernel,
        grid=(bsz, seq // tm),
        in_specs=[row, halo, vec, const(6, D_MODEL), mat, mat, mat, mat,
                  const(D_MODEL, RW_LORA), const(RW_LORA, D_MODEL),
                  const(D_MODEL, RW_LORA), const(RW_LORA, D_MODEL),
                  vec, vec, vec, vec, vec],
        out_specs=[slab_spec] * 6 + [row] * 2,
        out_shape=[slab] * 6 + [jax.ShapeDtypeStruct(x1.shape, BF16)] * 2,
        compiler_params=_cparams(("parallel", "arbitrary")),
        name="odd_in",
    )(x1, x1, ln.reshape(1, D_MODEL), mu, pw(wr), pw(wk), pw(wv), pw(wg),
      w1.astype(BF16), pw(w2), a1.astype(BF16), pw(a2),
      pv(w0), pv(a0), pv(k_k), pv(k_a), pv(r_k))


def _selection_matrices():
    n_rep = LANES // (2 * RW_HEADS)
    sel_k = np.zeros((2 * LANES, 8 * LANES), np.float32)
    sel_v = np.zeros((2 * LANES, 2 * LANES), np.float32)
    sel_y = np.zeros((2 * LANES, 2 * LANES), np.float32)
    for b in range(2):
        for i in range(8):
            for h in range(RW_HEADS):
                src = b * LANES + i * RW_HEADS + h
                for vl in range(n_rep):
                    sel_k[src, i * LANES + vl * 2 * RW_HEADS + b * RW_HEADS + h] = 1.0
                dst = (i // n_rep) * LANES + (i % n_rep) * 2 * RW_HEADS + b * RW_HEADS + h
                sel_v[src, dst] = 1.0
                sel_y[dst, src] = 1.0
    return jnp.asarray(sel_k, BF16), jnp.asarray(sel_v, BF16), jnp.asarray(sel_y, BF16)


def _split_bf16(x):
    hi = x.astype(BF16)
    return hi, (x - hi.astype(F32)).astype(BF16)


def _select(x, sel):
    hi, lo = _split_bf16(x)
    return (jnp.dot(hi, sel, preferred_element_type=F32) + jnp.dot(lo, sel, preferred_element_type=F32))


def _scan_kernel(g0, g1, b0, b1, k0, k1, r0, r1, a0, a1, v0, v1, selk_ref, selv_ref, sely_ref,
                 y_ref, rep, rep_g, g_rows0, g_rows1, v3, y3, st):
    tc = v0.shape[2]
    n_oct = tc // 8
    OP_B, OP_K, OP_R, OP_A = range(4)

    @pl.when(pl.program_id(0) == 0)
    def _():
        st[...] = jnp.zeros(st.shape, F32)

    selk = selk_ref[...]
    for op, (x0, x1) in enumerate(((b0, b1), (k0, k1), (r0, r1), (a0, a1))):
        for m in range(SCAN_SLABS):
            piece = jnp.concatenate([x0[0, m], x1[0, m]], axis=1)
            laid = jnp.dot(piece.astype(BF16), selk, preferred_element_type=F32)
            rep[op, m] = laid.reshape(n_oct, 8, 8 * LANES)
    for m in range(SCAN_SLABS):
        g_rows0[pl.ds(m, n_oct, stride=SCAN_SLABS), :] = g0[0, m, pl.ds(7, n_oct, stride=8), :]
        g_rows1[pl.ds(m, n_oct, stride=SCAN_SLABS), :] = g1[0, m, pl.ds(7, n_oct, stride=8), :]
    g_last = jnp.concatenate([g_rows0[...], g_rows1[...]], axis=1)
    rep_g[...] = _select(g_last, selk).reshape(n_oct, SCAN_SLABS, 8 * LANES)
    selv = selv_ref[...]
    for m in range(SCAN_SLABS):
        out = _select(jnp.concatenate([v0[0, m], v1[0, m]], axis=1), selv)
        v3[:, 2 * m, :] = out[:, 0:LANES]
        v3[:, 2 * m + 1, :] = out[:, LANES:2 * LANES]

    def row(op, k, o, s):
        m, kl = divmod(k, 8)
        return rep[op, m, o, s:s + 1, kl * LANES:(kl + 1) * LANES]

    def dot_state(op, o, s):
        acc = [jnp.zeros((SCAN_VHI, LANES), F32) for _ in range(2)]
        for k in range(RW_HEAD):
            acc[k % 2] = acc[k % 2] + st[k] * row(op, k, o, s)
        return acc[0] + acc[1]

    def octet(o, sa):
        o_next = jnp.minimum(o + 1, n_oct - 1)
        for s in range(8):
            t = o * 8 + s
            nxt = (o, s + 1) if s < 7 else (o_next, 0)
            vt = v3[t]
            accy = [jnp.zeros((SCAN_VHI, LANES), F32) for _ in range(2)]
            accs = [jnp.zeros((SCAN_VHI, LANES), F32) for _ in range(2)]
            for k in range(RW_HEAD):
                snew = st[k] + sa * row(OP_B, k, o, s) + vt * row(OP_K, k, o, s)
                st[k] = snew
                accy[k % 2] = accy[k % 2] + snew * row(OP_R, k, o, s)
                accs[k % 2] = accs[k % 2] + snew * row(OP_A, k, *nxt)
            y3[t] = accy[0] + accy[1]
            sa = accs[0] + accs[1]
        for k in range(RW_HEAD):
            m, kl = divmod(k, 8)
            st[k] = st[k] * rep_g[o, m:m + 1, kl * LANES:(kl + 1) * LANES]
        return sa

    lax.fori_loop(0, n_oct, octet, dot_state(OP_A, 0, 0))

    sely = sely_ref[...]
    for m in range(SCAN_SLABS):
        piece = jnp.concatenate([y3[:, 2 * m, :], y3[:, 2 * m + 1, :]], axis=1)
        both = _select(piece, sely)
        for b in range(2):
            y_ref[b, :, m * LANES:(m + 1) * LANES] = both[:, b * LANES:(b + 1) * LANES]


def _scan(r, w, k, v, a, b):
    bsz, _, seq, _ = r.shape
    assert bsz == 2
    tc = SCAN_CHUNK
    slab0 = pl.BlockSpec((1, SCAN_SLABS, tc, LANES), lambda i: (0, 0, i, 0))
    slab1 = pl.BlockSpec((1, SCAN_SLABS, tc, LANES), lambda i: (1, 0, i, 0))
    const = lambda *shape: pl.BlockSpec(shape, lambda i: (0,) * len(shape))
    sel_k, sel_v, sel_y = _selection_matrices()
    return pl.pallas_call(
        _scan_kernel,
        grid=(seq // tc,),
        in_specs=[slab0, slab1] * 6 + [const(*sel_k.shape), const(*sel_v.shape), const(*sel_y.shape)],
        out_specs=pl.BlockSpec((bsz, tc, D_MODEL), lambda i: (0, i, 0)),
        out_shape=jax.ShapeDtypeStruct((bsz, seq, D_MODEL), F32),
        scratch_shapes=[
            pltpu.VMEM((4, SCAN_SLABS, tc // 8, 8, 8 * LANES), F32),
            pltpu.VMEM((tc // 8, SCAN_SLABS, 8 * LANES), F32),
            pltpu.VMEM((tc // 8 * SCAN_SLABS, LANES), F32),
            pltpu.VMEM((tc // 8 * SCAN_SLABS, LANES), F32),
            pltpu.VMEM((tc, SCAN_VHI, LANES), F32),
            pltpu.VMEM((tc, SCAN_VHI, LANES), F32),
            pltpu.VMEM((RW_HEAD, SCAN_VHI, LANES), F32),
        ],
        compiler_params=_cparams(("arbitrary",)),
        name="scan",
    )(w, w, b, b, k, k, r, r, a, a, v, v, sel_k, sel_v, sel_y)


def _odd_out_kernel(y_ref, bonus_ref, g_ref, x_ref, lnw_ref, lnb_ref, wo_ref, fn_ref, o_ref):
    y = y_ref[0]
    inv_n = 1.0 / RW_HEAD
    dlt = y - _tile8(_head_sum(y)) * inv_n
    var = _tile8(_head_sum(dlt * dlt)) * inv_n
    yn = dlt * lax.rsqrt(var + LNX_EPS) * lnw_ref[...] + lnb_ref[...]
    z = ((yn + bonus_ref[0].astype(F32)) * g_ref[0].astype(F32)).astype(BF16)
    x2 = x_ref[0] + jnp.dot(z, wo_ref[...], preferred_element_type=F32)
    o_ref[0] = _rms(x2, fn_ref[...])


def _odd_out(y, bonus, gs, x1, lnw, lnb, wo, fn):
    bsz, seq, _ = x1.shape
    tm = RESID_TILE
    perm = _index_major()
    pv = lambda p: p.reshape(-1)[perm].reshape(1, D_MODEL)
    row = pl.BlockSpec((1, tm, D_MODEL), lambda b, i: (b, i, 0))
    vec = pl.BlockSpec((1, D_MODEL), lambda b, i: (0, 0))
    return pl.pallas_call(
        _odd_out_kernel,
        grid=(bsz, seq // tm),
        in_specs=[row, row, row, row, vec, vec, pl.BlockSpec((D_MODEL, D_MODEL), lambda b, i: (0, 0)), vec],
        out_specs=row,
        out_shape=jax.ShapeDtypeStruct(x1.shape, F32),
        compiler_params=_cparams(("parallel", "parallel")),
        name="odd_out",
    )(y, bonus, gs, x1, pv(lnw), pv(lnb), wo[perm, :].astype(BF16), fn.reshape(1, D_MODEL))


def kernel(x, ln_even, w_in_even, gm_norm, gm_ws, gm_b, w_out_even, ln_odd, rw_mu, rw_wr, rw_wk,
           rw_wv, rw_wg, rw_w0, rw_w1, rw_w2, rw_a0, rw_a1, rw_a2, rw_kk, rw_ka, rw_rk, rw_lnw,
           rw_lnb, rw_wo, final_norm):
    assert x.shape[1] % ATT_TILE == 0 and x.shape[2] == D_MODEL
    assert ln_even.shape[0] == 1 and ln_odd.shape[0] == 1
    ya, zb, q, k, v = _even_in(x, ln_even[0], w_in_even[0], gm_norm[0], gm_ws[0], gm_b[0])
    b_out = _attention(q, k, v)
    x1 = _even_out(x, ya, b_out, zb, w_out_even[0])
    r, w, kmod, vv, a, b, gs, bonus = _odd_in(
        x1, ln_odd[0], rw_mu[0], rw_wr[0], rw_wk[0], rw_wv[0], rw_wg[0], rw_w0[0], rw_w1[0], rw_w2[0],
        rw_a0[0], rw_a1[0], rw_a2[0], rw_kk[0], rw_ka[0], rw_rk[0])
    y = _scan(r, w, kmod, vv, a, b)
    return _odd_out(y, bonus, gs, x1, rw_lnw[0], rw_lnb[0], rw_wo[0], final_norm)
```

```python
import math

import numpy as np
import jax
import jax.numpy as jnp
from jax import lax
from jax.experimental import pallas as pl
from jax.experimental.pallas import tpu as pltpu

F32 = jnp.float32
BF16 = jnp.bfloat16

D_MODEL = 1024
GM_GROUPS = 4
GM_CH = 128
GM_CHUNK = 128
GM_WIDTH = GM_GROUPS * GM_CH
ATT_HEADS = 8
ATT_HD = 64
ATT_WIDTH = ATT_HEADS * ATT_HD
ATT_BLOCK = 128
DILATIONS = (1, 4, 16)
ATT_TILE = ATT_BLOCK * 16
ATT_SLABS = ATT_WIDTH // 128
ATT_UNROLL = 16
IN_EVEN = 2 * GM_WIDTH + 3 * ATT_WIDTH + D_MODEL
RW_HEAD = 64
RW_HEADS = D_MODEL // RW_HEAD
RW_LORA = 64
RMS_EPS = 1e-6
LNX_EPS = 64e-5

LANES = 128
TOKEN_TILE = 256
RESID_TILE = 512
SCAN_CHUNK = 128
SCAN_VHI = RW_HEAD // 4
SCAN_SLABS = D_MODEL // LANES
VMEM_LIMIT = 56 * 1024 * 1024


def _cparams(sem):
    return pltpu.CompilerParams(dimension_semantics=sem, vmem_limit_bytes=VMEM_LIMIT)


def _gelu_tanh(x):
    return 0.5 * x * (1.0 + jnp.tanh(0.7978845608028654 * (x + 0.044715 * (x * x * x))))


def _silu(x):
    return x * jax.nn.sigmoid(x)


def _rms(x, g):
    return x * lax.rsqrt(jnp.mean(x * x, axis=-1, keepdims=True) + RMS_EPS) * g


def _even_in_kernel(x_ref, ln_ref, win_ref, gmn_ref, ws_ref, gmb_ref,
                    ya_ref, zb_ref, q_ref, k_ref, v_ref):
    tm = x_ref.shape[1]
    h = _rms(x_ref[0], ln_ref[...])
    proj = jnp.dot(h.astype(BF16), win_ref[...], preferred_element_type=F32)
    row = lax.broadcasted_iota(jnp.int32, (GM_CHUNK, GM_CHUNK), 0)
    col = lax.broadcasted_iota(jnp.int32, (GM_CHUNK, GM_CHUNK), 1)
    causal = row >= col
    z0 = 2 * GM_WIDTH + 3 * ATT_WIDTH
    sz = _silu(proj[:, z0:z0 + D_MODEL])
    for g in range(GM_GROUPS):
        u = _gelu_tanh(proj[:, g * GM_CH:(g + 1) * GM_CH])
        vg = _gelu_tanh(proj[:, GM_WIDTH + g * GM_CH:GM_WIDTH + (g + 1) * GM_CH])
        vn = vg * lax.rsqrt(jnp.mean(vg * vg, axis=-1, keepdims=True) + RMS_EPS) * gmn_ref[g:g + 1, :]
        wsg = jnp.where(causal, ws_ref[g], 0.0).astype(BF16)
        for c in range(tm // GM_CHUNK):
            rows = slice(c * GM_CHUNK, (c + 1) * GM_CHUNK)
            sp = jnp.dot(wsg, vn[rows].astype(BF16), preferred_element_type=F32) + gmb_ref[g]
            a = u[rows] * sp
            ya_ref[0, rows, g * GM_CH:(g + 1) * GM_CH] = (a * sz[rows, g * GM_CH:(g + 1) * GM_CH]).astype(BF16)
    q0 = 2 * GM_WIDTH
    for j in range(ATT_SLABS):
        lanes = slice(j * LANES, (j + 1) * LANES)
        zb_ref[0, j] = sz[:, GM_WIDTH + j * LANES:GM_WIDTH + (j + 1) * LANES]
        q_ref[0, j] = proj[:, q0:q0 + ATT_WIDTH][:, lanes] * (1.0 / math.sqrt(ATT_HD))
        k_ref[0, j] = proj[:, q0 + ATT_WIDTH:q0 + 2 * ATT_WIDTH][:, lanes]
        v_ref[0, j] = proj[:, q0 + 2 * ATT_WIDTH:q0 + 3 * ATT_WIDTH][:, lanes]


def _even_in(x, ln, w_in, gm_norm, gm_ws, gm_b):
    bsz, seq, _ = x.shape
    tm = TOKEN_TILE
    slab = jax.ShapeDtypeStruct((bsz, ATT_SLABS, seq, LANES), F32)
    slab_spec = pl.BlockSpec((1, ATT_SLABS, tm, LANES), lambda b, i: (b, 0, i, 0))
    const = lambda *shape: pl.BlockSpec(shape, lambda b, i: (0,) * len(shape))
    gmb = jnp.broadcast_to(gm_b[:, :, None], (GM_GROUPS, GM_CHUNK, GM_CH))
    return pl.pallas_call(
        _even_in_kernel,
        grid=(bsz, seq // tm),
        in_specs=[
            pl.BlockSpec((1, tm, D_MODEL), lambda b, i: (b, i, 0)),
            const(1, D_MODEL),
            const(D_MODEL, IN_EVEN),
            const(GM_GROUPS, GM_CH),
            const(GM_GROUPS, GM_CHUNK, GM_CHUNK),
            const(GM_GROUPS, GM_CHUNK, GM_CH),
        ],
        out_specs=[
            pl.BlockSpec((1, tm, GM_WIDTH), lambda b, i: (b, i, 0)),
            slab_spec, slab_spec, slab_spec, slab_spec,
        ],
        out_shape=[jax.ShapeDtypeStruct((bsz, seq, GM_WIDTH), BF16), slab, slab, slab, slab],
        compiler_params=_cparams(("parallel", "parallel")),
        name="even_in",
    )(x, ln.reshape(1, D_MODEL), w_in.astype(BF16), gm_norm, gm_ws, gmb)


def _attn_table():
    qi = np.arange(ATT_BLOCK)[:, None]
    kloc = np.arange(2 * ATT_BLOCK)[None, :] - ATT_BLOCK
    j = qi - kloc
    valid = (j >= 0) & (j <= ATT_BLOCK)
    slopes = 2.0 ** (-8.0 * np.arange(1, ATT_HEADS + 1) / ATT_HEADS)
    tabs = []
    for d in DILATIONS:
        bias = -slopes[:, None, None].astype(np.float32) * (j * d).astype(np.float32)[None]
        later = np.where(valid[None], bias, -np.inf)
        first = np.where(kloc[None] >= 0, later, -np.inf)
        tabs.append(np.stack([later, first], axis=1))
    return jnp.asarray(np.stack(tabs), dtype=F32)


def _attn_kernel(q_ref, kp_ref, kc_ref, vp_ref, vc_ref, tab_ref, o_ref,
                 kbuf, vbuf, sbuf, pbuf, acc, mm, ll_sw):
    tile = pl.program_id(2)
    kbuf[0:ATT_TILE, :] = kp_ref[0, 0]
    kbuf[ATT_TILE:2 * ATT_TILE, :] = kc_ref[0, 0]
    vbuf[0:ATT_TILE, :] = vp_ref[0, 0]
    vbuf[ATT_TILE:2 * ATT_TILE, :] = vc_ref[0, 0]
    head0_q = lax.broadcasted_iota(jnp.int32, (ATT_BLOCK, LANES), 1) < ATT_HD
    head0_kv = lax.broadcasted_iota(jnp.int32, (2 * ATT_BLOCK, LANES), 1) < ATT_HD
    n_blk = ATT_TILE // ATT_BLOCK

    for p, d in enumerate(DILATIONS):
        shift = int(math.log2(d))

        def rows(i, d=d, shift=shift):
            nl = lax.shift_right_logical(i, shift)
            r = jnp.bitwise_and(i, d - 1)
            qs = nl * (ATT_BLOCK * d) + r
            ks = qs + (ATT_TILE - ATT_BLOCK * d)
            if d == 1:
                return nl, pl.ds(qs, ATT_BLOCK), pl.ds(ks, 2 * ATT_BLOCK)
            return nl, pl.ds(qs, ATT_BLOCK, stride=d), pl.ds(ks, 2 * ATT_BLOCK, stride=d)

        def scores(i, carry, p=p, rows=rows):
            nl, qsl, ksl = rows(i)
            qf = q_ref[0, 0, qsl, :]
            kf = kbuf[ksl, :].astype(BF16)
            first = jnp.logical_and(tile == 0, nl == 0).astype(jnp.int32)
            qm = jnp.concatenate([jnp.where(head0_q, qf, 0.0), jnp.where(head0_q, 0.0, qf)], axis=0)
            s = lax.dot_general(qm.astype(BF16), kf, (((1,), (1,)), ((), ())), preferred_element_type=F32)
            for hh in range(2):
                sbuf[i, hh] = s[hh * ATT_BLOCK:(hh + 1) * ATT_BLOCK] + tab_ref[p, hh, first]
            return carry

        def softmax(i, carry, p=p, rows=rows):
            _, qsl, _ = rows(i)
            m = []
            for hh in range(2):
                s = sbuf[i, hh]
                m.append(jnp.max(s, axis=-1, keepdims=True))
                pbuf[i, hh] = jnp.exp(s - m[hh]).astype(BF16)
            mm[p, qsl, :] = jnp.where(head0_q, m[0], m[1])
            return carry

        def values(i, carry, p=p, rows=rows):
            _, qsl, ksl = rows(i)
            vf = vbuf[ksl, :]
            o0 = jnp.dot(pbuf[i, 0], jnp.where(head0_kv, vf, 1.0).astype(BF16), preferred_element_type=F32)
            o1 = jnp.dot(pbuf[i, 1], jnp.where(head0_kv, 1.0, vf).astype(BF16), preferred_element_type=F32)
            acc[p, qsl, :] = jnp.where(head0_q, o0, o1)
            ll_sw[p, qsl, :] = jnp.where(head0_q, o1, o0)
            return carry

        for stage in (scores, softmax, values):
            lax.fori_loop(0, n_blk, stage, 0, unroll=ATT_UNROLL)

    m_all = jnp.maximum(jnp.maximum(mm[0], mm[1]), mm[2])
    num = jnp.zeros((ATT_TILE, LANES), F32)
    den = jnp.zeros((ATT_TILE, LANES), F32)
    for p in range(len(DILATIONS)):
        e = jnp.exp(mm[p] - m_all)
        num = num + e * acc[p]
        den = den + e * pltpu.roll(ll_sw[p], ATT_HD, 1)
    o_ref[0, 0] = num / den


def _attention(q, k, v):
    bsz, _, seq, _ = q.shape
    blk = (1, 1, ATT_TILE, LANES)
    cur = pl.BlockSpec(blk, lambda b, j, t: (b, j, t, 0))
    prev = pl.BlockSpec(blk, lambda b, j, t: (b, j, jnp.maximum(t - 1, 0), 0))
    n_pat = len(DILATIONS)
    return pl.pallas_call(
        _attn_kernel,
        grid=(bsz, ATT_SLABS, seq // ATT_TILE),
        in_specs=[cur, prev, cur, prev, cur,
                  pl.BlockSpec((n_pat, 2, 2, ATT_BLOCK, 2 * ATT_BLOCK), lambda b, j, t: (0, j, 0, 0, 0))],
        out_specs=cur,
        out_shape=jax.ShapeDtypeStruct(q.shape, F32),
        scratch_shapes=[
            pltpu.VMEM((2 * ATT_TILE, LANES), F32),
            pltpu.VMEM((2 * ATT_TILE, LANES), F32),
            pltpu.VMEM((ATT_TILE // ATT_BLOCK, 2, ATT_BLOCK, 2 * ATT_BLOCK), F32),
            pltpu.VMEM((ATT_TILE // ATT_BLOCK, 2, ATT_BLOCK, 2 * ATT_BLOCK), BF16),
            pltpu.VMEM((n_pat, ATT_TILE, LANES), F32),
            pltpu.VMEM((n_pat, ATT_TILE, LANES), F32),
            pltpu.VMEM((n_pat, ATT_TILE, LANES), F32),
        ],
        compiler_params=_cparams(("parallel", "parallel", "parallel")),
        name="attention",
    )(q, k, k, v, v, _attn_table())


def _even_out_kernel(x_ref, ya_ref, bo_ref, zb_ref, wout_ref, o_ref):
    parts = [ya_ref[0]] + [(bo_ref[0, j] * zb_ref[0, j]).astype(BF16) for j in range(ATT_SLABS)]
    y = jnp.concatenate(parts, axis=1)
    o_ref[0] = x_ref[0] + jnp.dot(y, wout_ref[...], preferred_element_type=F32)


def _even_out(x, ya, b_out, zb, w_out):
    bsz, seq, _ = x.shape
    tm = RESID_TILE
    row = pl.BlockSpec((1, tm, D_MODEL), lambda b, i: (b, i, 0))
    slab_spec = pl.BlockSpec((1, ATT_SLABS, tm, LANES), lambda b, i: (b, 0, i, 0))
    return pl.pallas_call(
        _even_out_kernel,
        grid=(bsz, seq // tm),
        in_specs=[row, pl.BlockSpec((1, tm, GM_WIDTH), lambda b, i: (b, i, 0)), slab_spec, slab_spec,
                  pl.BlockSpec((D_MODEL, D_MODEL), lambda b, i: (0, 0))],
        out_specs=row,
        out_shape=jax.ShapeDtypeStruct(x.shape, F32),
        compiler_params=_cparams(("parallel", "parallel")),
        name="even_out",
    )(x, ya, b_out, zb, w_out.astype(BF16))


def _index_major(n=RW_HEAD, heads=RW_HEADS):
    new = np.arange(n * heads)
    return (new % heads) * n + new // heads


def _head_sum(t):
    s = t[:, 0:LANES]
    for c in range(1, D_MODEL // LANES):
        s = s + t[:, c * LANES:(c + 1) * LANES]
    for sh in (RW_HEADS, 2 * RW_HEADS, 4 * RW_HEADS):
        s = s + pltpu.roll(s, sh, 1)
    return s


def _tile8(s):
    return jnp.concatenate([s] * (D_MODEL // LANES), axis=1)


def _odd_in_kernel(x_ref, halo_ref, ln_ref, mu_ref, wr_ref, wk_ref, wv_ref, wg_ref,
                   w1_ref, w2_ref, a1_ref, a2_ref, w0_ref, a0_ref, kk_ref, ka_ref, rk_ref,
                   r_out, w_out, k_out, v_out, a_out, b_out, g_out, bonus_out):
    tm = x_ref.shape[1]
    ln = ln_ref[...]
    h = _rms(x_ref[0], ln)
    h_halo = _rms(halo_ref[0], ln)
    last = jnp.where(pl.program_id(1) == 0, 0.0, h_halo[7:8, :])
    first_row = lax.broadcasted_iota(jnp.int32, (tm, D_MODEL), 0) == 0
    hprev = jnp.where(first_row, last, pltpu.roll(h, 1, 0))
    xx = hprev - h
    mix = lambda i: (h + xx * mu_ref[i:i + 1, :]).astype(BF16)
    dot = lambda a, b: jnp.dot(a, b, preferred_element_type=F32)
    r = dot(mix(0), wr_ref[...])
    lw = dot(jnp.tanh(dot(mix(1), w1_ref[...])).astype(BF16), w2_ref[...])
    k = dot(mix(2), wk_ref[...])
    v = dot(mix(3), wv_ref[...])
    la = dot(dot(mix(4), a1_ref[...]).astype(BF16), a2_ref[...])
    g = dot(mix(5), wg_ref[...])
    decay = jnp.exp(-math.exp(-0.5) * jax.nn.sigmoid(w0_ref[...] + lw))
    a = jax.nn.sigmoid(a0_ref[...] + la)
    kk = k * kk_ref[...]
    kk = kk * _tile8(lax.rsqrt(jnp.maximum(_head_sum(kk * kk), 1e-24)))
    kmod = k * (1.0 + (a - 1.0) * ka_ref[...])
    bonus = _tile8(_head_sum(r * kmod * rk_ref[...])) * v
    step = lax.broadcasted_iota(jnp.int32, (tm, LANES), 0)
    in_octet = jnp.bitwise_and(step, 7)
    keep = [in_octet >= sh for sh in (1, 2, 4)]
    chunk_start = jnp.bitwise_and(step, SCAN_CHUNK - 1) == 0
    kkb = kk * a
    for m in range(SCAN_SLABS):
        lanes = slice(m * LANES, (m + 1) * LANES)
        cum = decay[:, lanes]
        for sh, mask in zip((1, 2, 4), keep):
            cum = cum * jnp.where(mask, pltpu.roll(cum, sh, 0), 1.0)
        cum_before = jnp.where(chunk_start, 1.0, pltpu.roll(cum, 1, 0))
        inv_cum = 1.0 / cum
        r_out[0, m] = (r[:, lanes] * cum).astype(BF16)
        w_out[0, m] = cum
        k_out[0, m] = (kmod[:, lanes] * inv_cum).astype(BF16)
        v_out[0, m] = v[:, lanes]
        a_out[0, m] = (-(kk[:, lanes] * cum_before)).astype(BF16)
        b_out[0, m] = (kkb[:, lanes] * inv_cum).astype(BF16)
    g_out[0] = _silu(g).astype(BF16)
    bonus_out[0] = bonus.astype(BF16)


def _odd_in(x1, ln, mu, wr, wk, wv, wg, w0, w1, w2, a0, a1, a2, k_k, k_a, r_k):
    bsz, seq, _ = x1.shape
    tm = TOKEN_TILE
    assert tm % SCAN_CHUNK == 0
    perm = _index_major()
    pw = lambda w: w[:, perm].astype(BF16)
    pv = lambda p: p.reshape(-1)[perm].reshape(1, D_MODEL)
    row = pl.BlockSpec((1, tm, D_MODEL), lambda b, i: (b, i, 0))
    halo = pl.BlockSpec((1, 8, D_MODEL), lambda b, i: (b, jnp.maximum(i * (tm // 8) - 1, 0), 0))
    const = lambda *shape: pl.BlockSpec(shape, lambda b, i: (0,) * len(shape))
    mat, vec = const(D_MODEL, D_MODEL), const(1, D_MODEL)
    slab = jax.ShapeDtypeStruct((bsz, SCAN_SLABS, seq, LANES), F32)
    slab16 = jax.ShapeDtypeStruct((bsz, SCAN_SLABS, seq, LANES), BF16)
    slab_spec = pl.BlockSpec((1, SCAN_SLABS, tm, LANES), lambda b, i: (b, 0, i, 0))
    return pl.pallas_call(
        _odd_in_kernel,
        grid=(bsz, seq // tm),
        in_specs=[row, halo, vec, const(6, D_MODEL), mat, mat, mat, mat,
                  const(D_MODEL, RW_LORA), const(RW_LORA, D_MODEL),
                  const(D_MODEL, RW_LORA), const(RW_LORA, D_MODEL),
                  vec, vec, vec, vec, vec],
        out_specs=[slab_spec] * 6 + [row] * 2,
        out_shape=[slab16, slab, slab16, slab, slab16, slab16] + [jax.ShapeDtypeStruct(x1.shape, BF16)] * 2,
        compiler_params=_cparams(("parallel", "arbitrary")),
        name="odd_in",
    )(x1, x1, ln.reshape(1, D_MODEL), mu, pw(wr), pw(wk), pw(wv), pw(wg),
      w1.astype(BF16), pw(w2), a1.astype(BF16), pw(a2),
      pv(w0), pv(a0), pv(k_k), pv(k_a), pv(r_k))


def _selection_matrices():
    n_rep = LANES // (2 * RW_HEADS)
    sel_k = np.zeros((2 * LANES, 8 * LANES), np.float32)
    sel_v = np.zeros((2 * LANES, 2 * LANES), np.float32)
    sel_y = np.zeros((2 * LANES, 2 * LANES), np.float32)
    for b in range(2):
        for i in range(8):
            for h in range(RW_HEADS):
                src = b * LANES + i * RW_HEADS + h
                for vl in range(n_rep):
                    sel_k[src, i * LANES + vl * 2 * RW_HEADS + b * RW_HEADS + h] = 1.0
                dst = (i // n_rep) * LANES + (i % n_rep) * 2 * RW_HEADS + b * RW_HEADS + h
                sel_v[src, dst] = 1.0
                sel_y[dst, src] = 1.0
    return jnp.asarray(sel_k, BF16), jnp.asarray(sel_v, BF16), jnp.asarray(sel_y, BF16)


def _split_bf16(x):
    hi = x.astype(BF16)
    return hi, (x - hi.astype(F32)).astype(BF16)


def _select(x, sel):
    hi, lo = _split_bf16(x)
    return (jnp.dot(hi, sel, preferred_element_type=F32) + jnp.dot(lo, sel, preferred_element_type=F32))


def _scan_kernel(g0, g1, b0, b1, k0, k1, r0, r1, a0, a1, v0, v1, selk_ref, selv_ref, sely_ref,
                 y_ref, rep, rep_g, g_rows0, g_rows1, v3, y3, st):
    tc = v0.shape[2]
    n_oct = tc // 8
    OP_B, OP_K, OP_R, OP_A = range(4)

    @pl.when(pl.program_id(0) == 0)
    def _():
        st[...] = jnp.zeros(st.shape, F32)

    selk = selk_ref[...]
    for op, (x0, x1) in enumerate(((b0, b1), (k0, k1), (r0, r1), (a0, a1))):
        for m in range(SCAN_SLABS):
            piece = jnp.concatenate([x0[0, m], x1[0, m]], axis=1)
            laid = jnp.dot(piece, selk, preferred_element_type=F32)
            rep[op, m] = laid.reshape(n_oct, 8, 8 * LANES)
    for m in range(SCAN_SLABS):
        g_rows0[pl.ds(m, n_oct, stride=SCAN_SLABS), :] = g0[0, m, pl.ds(7, n_oct, stride=8), :]
        g_rows1[pl.ds(m, n_oct, stride=SCAN_SLABS), :] = g1[0, m, pl.ds(7, n_oct, stride=8), :]
    g_last = jnp.concatenate([g_rows0[...], g_rows1[...]], axis=1)
    rep_g[...] = _select(g_last, selk).reshape(n_oct, SCAN_SLABS, 8 * LANES)
    selv = selv_ref[...]
    for m in range(SCAN_SLABS):
        out = _select(jnp.concatenate([v0[0, m], v1[0, m]], axis=1), selv)
        v3[:, 2 * m, :] = out[:, 0:LANES]
        v3[:, 2 * m + 1, :] = out[:, LANES:2 * LANES]

    def row(op, k, o, s):
        m, kl = divmod(k, 8)
        return rep[op, m, o, s:s + 1, kl * LANES:(kl + 1) * LANES]

    def dot_state(op, o, s):
        acc = [jnp.zeros((SCAN_VHI, LANES), F32) for _ in range(2)]
        for k in range(RW_HEAD):
            acc[k % 2] = acc[k % 2] + st[k] * row(op, k, o, s)
        return acc[0] + acc[1]

    def octet(o, sa):
        o_next = jnp.minimum(o + 1, n_oct - 1)
        for s in range(8):
            t = o * 8 + s
            nxt = (o, s + 1) if s < 7 else (o_next, 0)
            vt = v3[t]
            accy = [jnp.zeros((SCAN_VHI, LANES), F32) for _ in range(2)]
            accs = [jnp.zeros((SCAN_VHI, LANES), F32) for _ in range(2)]
            for k in range(RW_HEAD):
                snew = st[k] + sa * row(OP_B, k, o, s) + vt * row(OP_K, k, o, s)
                st[k] = snew
                accy[k % 2] = accy[k % 2] + snew * row(OP_R, k, o, s)
                accs[k % 2] = accs[k % 2] + snew * row(OP_A, k, *nxt)
            y3[t] = accy[0] + accy[1]
            sa = accs[0] + accs[1]
        for k in range(RW_HEAD):
            m, kl = divmod(k, 8)
            st[k] = st[k] * rep_g[o, m:m + 1, kl * LANES:(kl + 1) * LANES]
        return sa

    lax.fori_loop(0, n_oct, octet, dot_state(OP_A, 0, 0))

    sely = sely_ref[...]
    for m in range(SCAN_SLABS):
        piece = jnp.concatenate([y3[:, 2 * m, :], y3[:, 2 * m + 1, :]], axis=1)
        both = _select(piece, sely)
        for b in range(2):
            y_ref[b, :, m * LANES:(m + 1) * LANES] = both[:, b * LANES:(b + 1) * LANES]


def _scan(r, w, k, v, a, b):
    bsz, _, seq, _ = r.shape
    assert bsz == 2
    tc = SCAN_CHUNK
    slab0 = pl.BlockSpec((1, SCAN_SLABS, tc, LANES), lambda i: (0, 0, i, 0))
    slab1 = pl.BlockSpec((1, SCAN_SLABS, tc, LANES), lambda i: (1, 0, i, 0))
    const = lambda *shape: pl.BlockSpec(shape, lambda i: (0,) * len(shape))
    sel_k, sel_v, sel_y = _selection_matrices()
    return pl.pallas_call(
        _scan_kernel,
        grid=(seq // tc,),
        in_specs=[slab0, slab1] * 6 + [const(*sel_k.shape), const(*sel_v.shape), const(*sel_y.shape)],
        out_specs=pl.BlockSpec((bsz, tc, D_MODEL), lambda i: (0, i, 0)),
        out_shape=jax.ShapeDtypeStruct((bsz, seq, D_MODEL), F32),
        scratch_shapes=[
            pltpu.VMEM((4, SCAN_SLABS, tc // 8, 8, 8 * LANES), F32),
            pltpu.VMEM((tc // 8, SCAN_SLABS, 8 * LANES), F32),
            pltpu.VMEM((tc // 8 * SCAN_SLABS, LANES), F32),
            pltpu.VMEM((tc // 8 * SCAN_SLABS, LANES), F32),
            pltpu.VMEM((tc, SCAN_VHI, LANES), F32),
            pltpu.VMEM((tc, SCAN_VHI, LANES), F32),
            pltpu.VMEM((RW_HEAD, SCAN_VHI, LANES), F32),
        ],
        compiler_params=_cparams(("arbitrary",)),
        name="scan",
    )(w, w, b, b, k, k, r, r, a, a, v, v, sel_k, sel_v, sel_y)


def _odd_out_kernel(y_ref, bonus_ref, g_ref, x_ref, lnw_ref, lnb_ref, wo_ref, fn_ref, o_ref):
    y = y_ref[0]
    inv_n = 1.0 / RW_HEAD
    dlt = y - _tile8(_head_sum(y)) * inv_n
    var = _tile8(_head_sum(dlt * dlt)) * inv_n
    yn = dlt * lax.rsqrt(var + LNX_EPS) * lnw_ref[...] + lnb_ref[...]
    z = ((yn + bonus_ref[0].astype(F32)) * g_ref[0].astype(F32)).astype(BF16)
    x2 = x_ref[0] + jnp.dot(z, wo_ref[...], preferred_element_type=F32)
    o_ref[0] = _rms(x2, fn_ref[...])


def _odd_out(y, bonus, gs, x1, lnw, lnb, wo, fn):
    bsz, seq, _ = x1.shape
    tm = RESID_TILE
    perm = _index_major()
    pv = lambda p: p.reshape(-1)[perm].reshape(1, D_MODEL)
    row = pl.BlockSpec((1, tm, D_MODEL), lambda b, i: (b, i, 0))
    vec = pl.BlockSpec((1, D_MODEL), lambda b, i: (0, 0))
    return pl.pallas_call(
        _odd_out_kernel,
        grid=(bsz, seq // tm),
        in_specs=[row, row, row, row, vec, vec, pl.BlockSpec((D_MODEL, D_MODEL), lambda b, i: (0, 0)), vec],
        out_specs=row,
        out_shape=jax.ShapeDtypeStruct(x1.shape, F32),
        compiler_params=_cparams(("parallel", "parallel")),
        name="odd_out",
    )(y, bonus, gs, x1, pv(lnw), pv(lnb), wo[perm, :].astype(BF16), fn.reshape(1, D_MODEL))


def kernel(x, ln_even, w_in_even, gm_norm, gm_ws, gm_b, w_out_even, ln_odd, rw_mu, rw_wr, rw_wk,
           rw_wv, rw_wg, rw_w0, rw_w1, rw_w2, rw_a0, rw_a1, rw_a2, rw_kk, rw_ka, rw_rk, rw_lnw,
           rw_lnb, rw_wo, final_norm):
    assert x.shape[1] % ATT_TILE == 0 and x.shape[2] == D_MODEL
    assert ln_even.shape[0] == 1 and ln_odd.shape[0] == 1
    ya, zb, q, k, v = _even_in(x, ln_even[0], w_in_even[0], gm_norm[0], gm_ws[0], gm_b[0])
    b_out = _attention(q, k, v)
    x1 = _even_out(x, ya, b_out, zb, w_out_even[0])
    r, w, kmod, vv, a, b, gs, bonus = _odd_in(
        x1, ln_odd[0], rw_mu[0], rw_wr[0], rw_wk[0], rw_wv[0], rw_wg[0], rw_w0[0], rw_w1[0], rw_w2[0],
        rw_a0[0], rw_a1[0], rw_a2[0], rw_kk[0], rw_ka[0], rw_rk[0])
    y = _scan(r, w, kmod, vv, a, b)
    return _odd_out(y, bonus, gs, x1, rw_lnw[0], rw_lnb[0], rw_wo[0], final_norm)
```
